```python
import math
import jax, jax.numpy as jnp
from jax import lax
import numpy as np

D_MODEL = 1024
BATCH = 8
SEQ = 4096
DEPTH = 1

POOL_WINDOWS = (2, 4, 8, 16)
N_POOL_GROUPS = len(POOL_WINDOWS)
POOL_WIDTH = D_MODEL // 2
POOL_GROUP = POOL_WIDTH // N_POOL_GROUPS

MLA_HEADS = 8
QK_NOPE = 64
QK_ROPE = 32
QK_DIM = QK_NOPE + QK_ROPE
V_HEAD = 64
MLA_WIDTH = MLA_HEADS * V_HEAD
Q_LORA = 256
KV_LORA = 128
ROPE_THETA = 10000.0
Q_BLOCK = 128

IN_WIDTH = POOL_WIDTH + Q_LORA + KV_LORA + QK_ROPE
MIX_WIDTH = POOL_WIDTH + MLA_WIDTH

N_EXPERTS = 64
TOP_K = 8
N_GROUPS = 8
TOPK_GROUPS = 4
D_EXPERT = 256
D_SHARED = 256
ROUTED_SCALE = 2.5
MOE_BLOCK = 128

ALPHA = (2 * DEPTH) ** 0.25
BETA = (8 * DEPTH) ** -0.25
LN_EPS = 1e-5
RMS_EPS = 1e-6

kernel_name = "hybrid_pool_mla_moe_deepnorm"


def layer_norm(x, g, b):
    xf = x.astype(jnp.float32)
    mu = jnp.mean(xf, axis=-1, keepdims=True)
    xc = xf - mu
    var = jnp.mean(xc * xc, axis=-1, keepdims=True)
    return (xc * lax.rsqrt(var + LN_EPS) * g + b).astype(x.dtype)


def rms_norm(x, g):
    xf = x.astype(jnp.float32)
    ms = jnp.mean(xf * xf, axis=-1, keepdims=True)
    return (xf * lax.rsqrt(ms + RMS_EPS) * g).astype(x.dtype)


def window_mean(u, w):
    B, S, C = u.shape
    left = w // 2
    right = w - 1 - left
    c = jnp.concatenate([jnp.zeros((B, 1, C), jnp.float32),
                         jnp.cumsum(u.astype(jnp.float32), axis=1)], axis=1)
    pos = jnp.arange(S)
    lo = jnp.clip(pos - left, 0, S)
    hi = jnp.clip(pos + right + 1, 0, S)
    count = (hi - lo).astype(jnp.float32)
    mean = (c[:, hi] - c[:, lo]) / count[None, :, None]
    return mean.astype(u.dtype)


def pool_mixer(u, w_pool, pool_scale):
    B, S, _ = u.shape
    ug = u.reshape(B, S, N_POOL_GROUPS, POOL_GROUP)
    pooled = jnp.stack([window_mean(ug[:, :, g], w) for g, w in enumerate(POOL_WINDOWS)], axis=2) - ug
    y = jnp.einsum('bsgc,gcd->bsgd', pooled, w_pool)
    return y.reshape(B, S, POOL_WIDTH) * pool_scale


def apply_rope(t, cos, sin):
    t1, t2 = jnp.split(t, 2, axis=-1)
    return jnp.concatenate([t1 * cos - t2 * sin, t2 * cos + t1 * sin], axis=-1)


def mla(q_lat, kv_lat, k_rope_in, q_norm_g, w_q_up, kv_norm_g, w_kv_up):
    B, S, _ = q_lat.shape
    q = (rms_norm(q_lat, q_norm_g) @ w_q_up).reshape(B, S, MLA_HEADS, QK_DIM)
    kv = (rms_norm(kv_lat, kv_norm_g) @ w_kv_up).reshape(B, S, MLA_HEADS, QK_NOPE + V_HEAD)
    k_nope, v = kv[..., :QK_NOPE], kv[..., QK_NOPE:]

    pos = jnp.arange(S, dtype=jnp.float32)
    inv_freq = ROPE_THETA ** (-jnp.arange(0, QK_ROPE, 2, dtype=jnp.float32) / QK_ROPE)
    ang = pos[:, None] * inv_freq[None, :]
    cos = jnp.cos(ang)[:, None, :].astype(q.dtype)
    sin = jnp.sin(ang)[:, None, :].astype(q.dtype)

    q = jnp.concatenate([q[..., :QK_NOPE], apply_rope(q[..., QK_NOPE:], cos, sin)], axis=-1)
    k_rope = apply_rope(k_rope_in[:, :, None, :], cos, sin)
    k = jnp.concatenate([k_nope, jnp.broadcast_to(k_rope, (B, S, MLA_HEADS, QK_ROPE))], axis=-1)

    scale = QK_DIM ** -0.5
    n_blk = S // Q_BLOCK
    q_blocks = q.reshape(B, n_blk, Q_BLOCK, MLA_HEADS, QK_DIM).transpose(1, 0, 2, 3, 4)

    def attend(qb):
        s = jnp.einsum('bqhd,bkhd->bhqk', qb, k, preferred_element_type=jnp.float32) * scale
        p = jax.nn.softmax(s, axis=-1).astype(v.dtype)
        return jnp.einsum('bhqk,bkhd->bqhd', p, v)

    o = lax.map(attend, q_blocks)
    return o.transpose(1, 0, 2, 3, 4).reshape(B, S, MLA_WIDTH)


def moe(h, w_router, router_bias, w_gate, w_up, w_down, w_sh_gate, w_sh_up, w_sh_down):
    B, S, D = h.shape
    xf = h.reshape(-1, D)
    N = xf.shape[0]
    NK = N * TOP_K

    logits = jnp.einsum('nd,de->ne', xf, w_router, preferred_element_type=jnp.float32)
    scores = jax.nn.sigmoid(logits)
    biased = scores + router_bias.astype(jnp.float32)
    grp_score = lax.top_k(biased.reshape(N, N_GROUPS, N_EXPERTS // N_GROUPS), 2)[0].sum(-1)
    _, top_grp = lax.top_k(grp_score, TOPK_GROUPS)
    grp_mask = jnp.any(top_grp[..., None] == jnp.arange(N_GROUPS), axis=1)
    expert_mask = jnp.repeat(grp_mask, N_EXPERTS // N_GROUPS, axis=1)
    _, top_idx = lax.top_k(jnp.where(expert_mask, biased, -jnp.inf), TOP_K)
    top_w = jnp.take_along_axis(scores, top_idx, axis=1)
    top_w = top_w / jnp.sum(top_w, axis=-1, keepdims=True) * ROUTED_SCALE

    flat_e = top_idx.reshape(-1)
    order = jnp.argsort(flat_e)
    sorted_e = flat_e[order]
    tok = order // TOP_K
    sizes = jnp.bincount(flat_e, length=N_EXPERTS)
    padded = (sizes + MOE_BLOCK - 1) // MOE_BLOCK * MOE_BLOCK
    pad_end = jnp.cumsum(padded)
    pad_start = pad_end - padded
    start = jnp.cumsum(sizes) - sizes
    dest = pad_start[sorted_e] + jnp.arange(NK) - start[sorted_e]
    n_blocks = -(-(NK + N_EXPERTS * (MOE_BLOCK - 1)) // MOE_BLOCK)
    P = n_blocks * MOE_BLOCK

    slot_tok = jnp.full((P,), N, jnp.int32).at[dest].set(tok.astype(jnp.int32))
    slot_w = jnp.zeros((P,), jnp.float32).at[dest].set(top_w.reshape(-1)[order])
    x_pad = jnp.concatenate([xf, jnp.zeros((1, D), xf.dtype)], axis=0)
    buf = x_pad[slot_tok].reshape(n_blocks, MOE_BLOCK, D)
    block_e = jnp.minimum(jnp.searchsorted(pad_end, jnp.arange(n_blocks) * MOE_BLOCK, side='right'),
                          N_EXPERTS - 1)

    def expert_block(args):
        xb, e = args
        return (jax.nn.silu(xb @ w_gate[e]) * (xb @ w_up[e])) @ w_down[e]

    out_buf = lax.map(expert_block, (buf, block_e)).reshape(P, D)
    routed = jnp.zeros((N + 1, D), xf.dtype).at[slot_tok].add(
        out_buf * slot_w[:, None].astype(xf.dtype))[:N]
    shared = (jax.nn.silu(xf @ w_sh_gate) * (xf @ w_sh_up)) @ w_sh_down
    return (routed + shared).reshape(B, S, D)


def setup_inputs(seed: int = 0) -> dict:
    key = jax.random.key(seed)
    ks = jax.random.split(key, 24)
    L = DEPTH

    def nrm(k, shape, scale):
        return jax.random.normal(k, shape, jnp.float32) * scale

    v_scale = jnp.concatenate([jnp.ones((QK_NOPE,), jnp.float32), jnp.full((V_HEAD,), BETA, jnp.float32)])
    w_kv_up = (nrm(ks[6], (L, KV_LORA, MLA_HEADS, QK_NOPE + V_HEAD), KV_LORA ** -0.5) * v_scale
               ).reshape(L, KV_LORA, MLA_HEADS * (QK_NOPE + V_HEAD))
    return {
        "x": nrm(ks[0], (BATCH, SEQ, D_MODEL), 1.0),
        "w_in": nrm(ks[1], (L, D_MODEL, IN_WIDTH), D_MODEL ** -0.5),
        "w_pool": nrm(ks[2], (L, N_POOL_GROUPS, POOL_GROUP, POOL_GROUP), POOL_GROUP ** -0.5),
        "pool_scale": 1.0 + nrm(ks[3], (L, POOL_WIDTH), 0.1),
        "q_norm_g": 1.0 + nrm(ks[4], (L, Q_LORA), 0.02),
        "w_q_up": nrm(ks[5], (L, Q_LORA, MLA_HEADS * QK_DIM), Q_LORA ** -0.5),
        "kv_norm_g": 1.0 + nrm(ks[7], (L, KV_LORA), 0.02),
        "w_kv_up": w_kv_up,
        "w_out": nrm(ks[8], (L, MIX_WIDTH, D_MODEL), MIX_WIDTH ** -0.5 * BETA),
        "ln1_g": 1.0 + nrm(ks[9], (L, D_MODEL), 0.02),
        "ln1_b": nrm(ks[10], (L, D_MODEL), 0.02),
        "w_router": nrm(ks[11], (L, D_MODEL, N_EXPERTS), D_MODEL ** -0.5),
        "router_bias": nrm(ks[12], (L, N_EXPERTS), 0.01),
        "w_gate": nrm(ks[13], (L, N_EXPERTS, D_MODEL, D_EXPERT), D_MODEL ** -0.5),
        "w_up": nrm(ks[14], (L, N_EXPERTS, D_MODEL, D_EXPERT), D_MODEL ** -0.5),
        "w_down": nrm(ks[15], (L, N_EXPERTS, D_EXPERT, D_MODEL), D_EXPERT ** -0.5 * BETA),
        "w_sh_gate": nrm(ks[16], (L, D_MODEL, D_SHARED), D_MODEL ** -0.5),
        "w_sh_up": nrm(ks[17], (L, D_MODEL, D_SHARED), D_MODEL ** -0.5),
        "w_sh_down": nrm(ks[18], (L, D_SHARED, D_MODEL), D_SHARED ** -0.5 * BETA),
        "ln2_g": 1.0 + nrm(ks[19], (L, D_MODEL), 0.02),
        "ln2_b": nrm(ks[20], (L, D_MODEL), 0.02),
    }


def reference(x, w_in, w_pool, pool_scale, q_norm_g, w_q_up, kv_norm_g, w_kv_up, w_out,
              ln1_g, ln1_b, w_router, router_bias, w_gate, w_up, w_down,
              w_sh_gate, w_sh_up, w_sh_down, ln2_g, ln2_b):
    h = x
    splits = [POOL_WIDTH, POOL_WIDTH + Q_LORA, POOL_WIDTH + Q_LORA + KV_LORA]
    for l in range(DEPTH):
        proj = h @ w_in[l]
        u_pool, q_lat, kv_lat, k_rope = jnp.split(proj, splits, axis=-1)
        y_pool = pool_mixer(u_pool, w_pool[l], pool_scale[l])
        y_mla = mla(q_lat, kv_lat, k_rope, q_norm_g[l], w_q_up[l], kv_norm_g[l], w_kv_up[l])
        mixed = jnp.concatenate([y_pool, y_mla], axis=-1) @ w_out[l]
        h = layer_norm(ALPHA * h + mixed, ln1_g[l], ln1_b[l])
        f = moe(h, w_router[l], router_bias[l], w_gate[l], w_up[l], w_down[l],
                w_sh_gate[l], w_sh_up[l], w_sh_down[l])
        h = layer_norm(ALPHA * h + f, ln2_g[l], ln2_b[l])
    return h
```

```python
import functools

import jax
import jax.numpy as jnp
from jax import lax
from jax.experimental import pallas as pl
from jax.experimental.pallas import tpu as pltpu

F32 = jnp.float32
BF16 = jnp.bfloat16
I32 = jnp.int32
U32 = jnp.uint32

D_MODEL = 1024
POOL_WINDOWS = (2, 4, 8, 16)
POOL_WIDTH = 512
POOL_GROUP = 128
MLA_HEADS = 8
QK_NOPE = 64
QK_ROPE = 32
QK_DIM = QK_NOPE + QK_ROPE
V_HEAD = 64
Q_LORA = 256
KV_LORA = 128
ROPE_THETA = 10000.0
HEAD_PAD = 128
HALO = 8

N_EXPERTS = 64
TOP_K = 8
N_GROUPS = 8
GROUP_SIZE = N_EXPERTS // N_GROUPS
TOPK_GROUPS = 4
D_EXPERT = 256
D_SHARED = 256
ROUTED_SCALE = 2.5

DEPTH = 1
ALPHA = (2 * DEPTH) ** 0.25
LN_EPS = 1e-5
RMS_EPS = 1e-6

VMEM_LIMIT = 56 * 1024 * 1024

TS = 512
TQ = 256
TM = 256
BM = 256
HALF = D_MODEL // 2

NT_DIMS = (((1,), (1,)), ((), ()))


def _pack_halves(lo, hi):
    ulo = lax.bitcast_convert_type(lo.astype(BF16).astype(F32), U32)
    uhi = lax.bitcast_convert_type(hi.astype(BF16).astype(F32), U32)
    return ulo | (uhi >> 16)


def _unpack_halves(p):
    lo = lax.bitcast_convert_type(p & jnp.uint32(0xFFFF0000), F32)
    hi = lax.bitcast_convert_type(p << 16, F32)
    return lo, hi


def _proj_kernel(x_ref, xl_ref, xr_ref, win_ref, wpool_ref, pscale_ref, qg_ref, wq_ref, kvg_ref,
                 wkv_ref, cos_ref, sin_ref, ypool_ref, q_ref, k_ref, v_ref, uext_ref, *, seq_len):
    i = pl.program_id(1)
    n_i = pl.num_programs(1)
    ts = x_ref.shape[1]

    proj = jnp.dot(x_ref[0].astype(BF16), win_ref[...], preferred_element_type=F32)
    u = proj[:, :POOL_WIDTH]

    w_u = win_ref[:, :POOL_WIDTH]
    ul = jnp.dot(xl_ref[0].astype(BF16), w_u, preferred_element_type=F32)
    ur = jnp.dot(xr_ref[0].astype(BF16), w_u, preferred_element_type=F32)
    uext_ref[0:HALO, :] = jnp.where(i > 0, ul, 0.0)
    uext_ref[HALO:HALO + ts, :] = u
    uext_ref[HALO + ts:HALO + ts + HALO, :] = jnp.where(i < n_i - 1, ur, 0.0)

    pos = lax.broadcasted_iota(I32, (ts, 1), 0) + i * ts
    outs = []
    for g, w in enumerate(POOL_WINDOWS):
        left = w // 2
        right = w - 1 - left
        c0, c1 = g * POOL_GROUP, (g + 1) * POOL_GROUP
        acc = uext_ref[HALO - left:HALO - left + ts, c0:c1]
        for j in range(-left + 1, right + 1):
            acc = acc + uext_ref[HALO + j:HALO + j + ts, c0:c1]
        lo = jnp.maximum(pos - left, 0)
        hi = jnp.minimum(pos + right + 1, seq_len)
        pooled = acc / (hi - lo).astype(F32) - u[:, c0:c1]
        outs.append(jnp.dot(pooled.astype(BF16), wpool_ref[g], preferred_element_type=F32))
    ypool_ref[0] = (jnp.concatenate(outs, axis=1) * pscale_ref[...]).astype(BF16)

    cosf = cos_ref[...]
    sinf = sin_ref[...]

    ql = proj[:, POOL_WIDTH:POOL_WIDTH + Q_LORA]
    qn = ql * lax.rsqrt(jnp.mean(ql * ql, axis=-1, keepdims=True) + RMS_EPS) * qg_ref[...]
    qq = jnp.dot(qn.astype(BF16), wq_ref[...], preferred_element_type=F32)
    scale = QK_DIM ** -0.5
    hw = MLA_HEADS * HEAD_PAD
    for h in range(MLA_HEADS):
        qa = qq[:, h * HEAD_PAD:(h + 1) * HEAD_PAD]
        qb = qq[:, hw + h * HEAD_PAD:hw + (h + 1) * HEAD_PAD]
        q_ref[0, h] = ((qa * cosf + qb * sinf) * scale).astype(BF16)

    c_kv = POOL_WIDTH + Q_LORA
    kvl = proj[:, c_kv:c_kv + KV_LORA]
    kvn = kvl * lax.rsqrt(jnp.mean(kvl * kvl, axis=-1, keepdims=True) + RMS_EPS) * kvg_ref[...]
    kv = jnp.dot(kvn.astype(BF16), wkv_ref[...], preferred_element_type=F32)
    c_r = c_kv + KV_LORA
    kr = proj[:, c_r:c_r + HEAD_PAD] * cosf + proj[:, c_r + HEAD_PAD:c_r + 2 * HEAD_PAD] * sinf
    for h in range(MLA_HEADS):
        k_ref[0, h] = (kv[:, h * HEAD_PAD:(h + 1) * HEAD_PAD] + kr).astype(BF16)
    v_ref[0] = kv[:, hw:].astype(BF16)


def _attn_kernel(q_ref, k_ref, v_ref, o_ref):
    tq = q_ref.shape[2]
    quad = 4 * V_HEAD
    lane = lax.broadcasted_iota(I32, (tq, quad), 1)
    for j in range(MLA_HEADS // 4):
        vq = v_ref[0, :, j * quad:(j + 1) * quad]
        acc = jnp.zeros((tq, quad), F32)
        for hh in range(4):
            h = 4 * j + hh
            s = lax.dot_general(q_ref[0, h], k_ref[0, h], NT_DIMS, preferred_element_type=F32)
            m = jnp.max(s, axis=1, keepdims=True)
            p = jnp.exp(s - m)
            l = jnp.sum(p, axis=1, keepdims=True)
            o = jnp.dot(p.astype(BF16), vq, preferred_element_type=F32) / l
            acc = jnp.where((lane >= hh * V_HEAD) & (lane < (hh + 1) * V_HEAD), o, acc)
        o_ref[0, :, j * quad:(j + 1) * quad] = acc.astype(BF16)


def _first_argmax(vals, iota, sentinel):
    m = jnp.max(vals, axis=0, keepdims=True)
    idx = jnp.min(jnp.where(vals == m, iota, sentinel), axis=0, keepdims=True)
    return m, idx


def _mix_kernel(x_ref, yp_ref, ym_ref, wout_ref, g1_ref, b1_ref, wrh_ref, wrl_ref, rb_ref, wsgu_ref,
                wsd_ref, base_ref, xp_ref, te_ref, tr_ref, tw_ref, cnt_ref, carry_ref):
    i = pl.program_id(0)
    tm = x_ref.shape[0]

    @pl.when(i == 0)
    def _():
        carry_ref[...] = jnp.zeros_like(carry_ref)

    mixed = (jnp.dot(yp_ref[...], wout_ref[:POOL_WIDTH, :], preferred_element_type=F32)
             + jnp.dot(ym_ref[...], wout_ref[POOL_WIDTH:, :], preferred_element_type=F32))
    z = ALPHA * x_ref[...] + mixed
    mu = jnp.mean(z, axis=-1, keepdims=True)
    zc = z - mu
    var = jnp.mean(zc * zc, axis=-1, keepdims=True)
    h = zc * lax.rsqrt(var + LN_EPS) * g1_ref[...] + b1_ref[...]
    hb = h.astype(BF16)

    gu = jnp.dot(hb, wsgu_ref[...], preferred_element_type=F32)
    act = jax.nn.silu(gu[:, :D_SHARED]) * gu[:, D_SHARED:]
    shared = jnp.dot(act.astype(BF16), wsd_ref[...], preferred_element_type=F32)
    base_ref[...] = ALPHA * h + shared
    xp_ref[...] = _pack_halves(h[:, :HALF], h[:, HALF:])

    h_lo = (h - hb.astype(F32)).astype(BF16)
    logits = (lax.dot_general(wrh_ref[...], hb, NT_DIMS, preferred_element_type=F32)
              + lax.dot_general(wrl_ref[...], hb, NT_DIMS, preferred_element_type=F32)
              + lax.dot_general(wrh_ref[...], h_lo, NT_DIMS, preferred_element_type=F32))
    scores = jax.nn.sigmoid(logits)
    biased = scores + rb_ref[...]

    neg_inf = jnp.float32(-jnp.inf)
    iota_g = lax.broadcasted_iota(I32, (GROUP_SIZE, tm), 0)
    grp_rows = []
    for g in range(N_GROUPS):
        bg = biased[g * GROUP_SIZE:(g + 1) * GROUP_SIZE, :]
        m1, i1 = _first_argmax(bg, iota_g, GROUP_SIZE)
        m2 = jnp.max(jnp.where(iota_g == i1, neg_inf, bg), axis=0, keepdims=True)
        grp_rows.append(m1 + m2)
    grp_score = jnp.concatenate(grp_rows, axis=0)

    iota_n = lax.broadcasted_iota(I32, (N_GROUPS, tm), 0)
    grp_sel = jnp.zeros((N_GROUPS, tm), jnp.bool_)
    for _ in range(TOPK_GROUPS):
        _, gi = _first_argmax(grp_score, iota_n, N_GROUPS)
        hit = iota_n == gi
        grp_sel = grp_sel | hit
        grp_score = jnp.where(hit, neg_inf, grp_score)

    cand = jnp.concatenate(
        [jnp.where(grp_sel[g:g + 1, :], biased[g * GROUP_SIZE:(g + 1) * GROUP_SIZE, :], neg_inf)
         for g in range(N_GROUPS)], axis=0)

    iota_e = lax.broadcasted_iota(I32, (N_EXPERTS, tm), 0)
    sel = jnp.zeros((N_EXPERTS, tm), jnp.bool_)
    e_rows, w_rows = [], []
    for _ in range(TOP_K):
        _, ei = _first_argmax(cand, iota_e, N_EXPERTS)
        hit = iota_e == ei
        e_rows.append(ei)
        w_rows.append(jnp.sum(jnp.where(hit, scores, 0.0), axis=0, keepdims=True))
        sel = sel | hit
        cand = jnp.where(hit, neg_inf, cand)
    top_e = jnp.concatenate(e_rows, axis=0)
    top_s = jnp.concatenate(w_rows, axis=0)
    top_w = top_s / jnp.sum(top_s, axis=0, keepdims=True) * ROUTED_SCALE

    sel_f = sel.astype(F32)
    r_i = lax.broadcasted_iota(I32, (tm, tm), 0)
    c_i = lax.broadcasted_iota(I32, (tm, tm), 1)
    before = (r_i < c_i).astype(BF16)
    rank = jnp.dot(sel_f.astype(BF16), before, preferred_element_type=F32) + carry_ref[:, 0:1]
    r_rows = [jnp.sum(jnp.where(iota_e == e_rows[kk], rank, 0.0), axis=0, keepdims=True)
              for kk in range(TOP_K)]
    new_carry = carry_ref[...] + jnp.sum(sel_f, axis=1, keepdims=True)
    carry_ref[...] = new_carry
    cnt_ref[...] = new_carry

    te_ref[...] = top_e
    tr_ref[...] = jnp.concatenate(r_rows, axis=0).astype(I32)
    tw_ref[...] = top_w


def _dest_kernel(ps_ref, te_ref, tr_ref, d_ref):
    te = te_ref[...]
    acc = tr_ref[...]
    for e in range(N_EXPERTS):
        acc = acc + jnp.where(te == e, ps_ref[e], 0)
    d_ref[...] = acc


def _dispatch_kernel(dest_ref, xp_ref, zeros_ref, buf_ref, idx_ref, isem, sem):
    del zeros_ref
    i = pl.program_id(0)
    tm = xp_ref.shape[0]
    cp = pltpu.make_async_copy(dest_ref.at[i], idx_ref, isem)
    cp.start()
    cp.wait()

    def row_copy(t, kk):
        d = idx_ref[kk * tm + t]
        return pltpu.make_async_copy(xp_ref.at[pl.ds(t, 1)], buf_ref.at[pl.ds(d, 1)], sem)

    def start(t, c):
        for kk in range(TOP_K):
            row_copy(t, kk).start()
        return c

    def wait(t, c):
        for kk in range(TOP_K):
            row_copy(t, kk).wait()
        return c

    lax.fori_loop(0, tm, start, 0)
    lax.fori_loop(0, tm, wait, 0)


def _ffn_kernel(be_ref, x_ref, wgu_ref, wd_ref, o_ref):
    del be_ref
    lo, hi = _unpack_halves(x_ref[...])
    x = jnp.concatenate([lo.astype(BF16), hi.astype(BF16)], axis=1)
    gu = jnp.dot(x, wgu_ref[0], preferred_element_type=F32)
    act = jax.nn.silu(gu[:, :D_EXPERT]) * gu[:, D_EXPERT:]
    y = jnp.dot(act.astype(BF16), wd_ref[0], preferred_element_type=F32)
    o_ref[...] = _pack_halves(y[:, :HALF], y[:, HALF:])


def _combine_kernel(dest_ref, base_ref, tw_ref, g2_ref, b2_ref, obuf_ref, out_ref, rows_ref, idx_ref,
                    isem, sem):
    i = pl.program_id(0)
    tm = base_ref.shape[0]
    cp = pltpu.make_async_copy(dest_ref.at[i], idx_ref, isem)
    cp.start()
    cp.wait()

    def row_copy(t, kk):
        d = idx_ref[kk * tm + t]
        return pltpu.make_async_copy(obuf_ref.at[pl.ds(d, 1)], rows_ref.at[kk, pl.ds(t, 1)], sem)

    def start(t, c):
        for kk in range(TOP_K):
            row_copy(t, kk).start()
        return c

    def wait(t, c):
        for kk in range(TOP_K):
            row_copy(t, kk).wait()
        return c

    lax.fori_loop(0, tm, start, 0)
    lax.fori_loop(0, tm, wait, 0)

    acc_lo = jnp.zeros((tm, HALF), F32)
    acc_hi = jnp.zeros((tm, HALF), F32)
    for kk in range(TOP_K):
        lo, hi = _unpack_halves(rows_ref[kk])
        w = tw_ref[:, kk:kk + 1]
        acc_lo = acc_lo + w * lo
        acc_hi = acc_hi + w * hi
    z = base_ref[...] + jnp.concatenate([acc_lo, acc_hi], axis=1)
    mu = jnp.mean(z, axis=-1, keepdims=True)
    zc = z - mu
    var = jnp.mean(zc * zc, axis=-1, keepdims=True)
    out_ref[...] = zc * lax.rsqrt(var + LN_EPS) * g2_ref[...] + b2_ref[...]


def _full(shape):
    nd = len(shape)
    return pl.BlockSpec(shape, lambda *_: (0,) * nd)


def _params(*sem):
    return pltpu.CompilerParams(dimension_semantics=sem, vmem_limit_bytes=VMEM_LIMIT)


def kernel(x, w_in, w_pool, pool_scale, q_norm_g, w_q_up, kv_norm_g, w_kv_up, w_out, ln1_g, ln1_b,
           w_router, router_bias, w_gate, w_up, w_down, w_sh_gate, w_sh_up, w_sh_down, ln2_g, ln2_b):
    B, S, D = x.shape
    assert D == D_MODEL and S % TS == 0 and S % TQ == 0 and (B * S) % TM == 0 and TS % HALO == 0
    N = B * S
    H = MLA_HEADS
    hw = H * HEAD_PAD

    wi = w_in[0]
    c_r = POOL_WIDTH + Q_LORA + KV_LORA
    rope = wi[:, c_r:c_r + QK_ROPE]
    half = QK_ROPE // 2
    zc64 = jnp.zeros((D, QK_NOPE), F32)
    zc32 = jnp.zeros((D, HEAD_PAD - QK_DIM), F32)
    rope_a = jnp.concatenate([zc64, rope, zc32], axis=1)
    rope_b = jnp.concatenate([zc64, -rope[:, half:], rope[:, :half], zc32], axis=1)
    win_p = jnp.concatenate([wi[:, :c_r], rope_a, rope_b], axis=1).astype(BF16)

    wq = w_q_up[0].reshape(Q_LORA, H, QK_DIM)
    zq = jnp.zeros((Q_LORA, H, HEAD_PAD - QK_DIM), F32)
    wq_a = jnp.concatenate([wq, zq], axis=2).reshape(Q_LORA, hw)
    wq_b = jnp.concatenate([jnp.zeros((Q_LORA, H, QK_NOPE), F32), -wq[:, :, QK_NOPE + half:],
                            wq[:, :, QK_NOPE:QK_NOPE + half], zq], axis=2).reshape(Q_LORA, hw)
    wq_p = jnp.concatenate([wq_a, wq_b], axis=1).astype(BF16)

    wkv = w_kv_up[0].reshape(KV_LORA, H, QK_NOPE + V_HEAD)
    wk = jnp.concatenate([wkv[:, :, :QK_NOPE], jnp.zeros((KV_LORA, H, HEAD_PAD - QK_NOPE), F32)],
                         axis=2).reshape(KV_LORA, hw)
    wv = wkv[:, :, QK_NOPE:].reshape(KV_LORA, H * V_HEAD)
    wkv_p = jnp.concatenate([wk, wv], axis=1).astype(BF16)

    pos = jnp.arange(S, dtype=F32)
    inv_freq = ROPE_THETA ** (-jnp.arange(0, QK_ROPE, 2, dtype=F32) / QK_ROPE)
    ang = pos[:, None] * inv_freq[None, :]
    cosv, sinv = jnp.cos(ang), jnp.sin(ang)
    cos_t = jnp.concatenate([jnp.ones((S, QK_NOPE), F32), cosv, cosv,
                             jnp.zeros((S, HEAD_PAD - QK_DIM), F32)], axis=1)
    sin_t = jnp.concatenate([jnp.zeros((S, QK_NOPE), F32), sinv, sinv,
                             jnp.zeros((S, HEAD_PAD - QK_DIM), F32)], axis=1)

    n_ts = S // TS
    hb = TS // HALO
    ypool, q, k, v = pl.pallas_call(
        functools.partial(_proj_kernel, seq_len=S),
        grid=(B, n_ts),
        in_specs=[
            pl.BlockSpec((1, TS, D), lambda b, i: (b, i, 0)),
            pl.BlockSpec((1, HALO, D), lambda b, i: (b, jnp.maximum(i * hb - 1, 0), 0)),
            pl.BlockSpec((1, HALO, D), lambda b, i: (b, jnp.minimum((i + 1) * hb, S // HALO - 1), 0)),
            _full(win_p.shape), _full((len(POOL_WINDOWS), POOL_GROUP, POOL_GROUP)),
            _full((1, POOL_WIDTH)), _full((1, Q_LORA)), _full(wq_p.shape), _full((1, KV_LORA)),
            _full(wkv_p.shape),
            pl.BlockSpec((TS, HEAD_PAD), lambda b, i: (i, 0)),
            pl.BlockSpec((TS, HEAD_PAD), lambda b, i: (i, 0)),
        ],
        out_specs=[
            pl.BlockSpec((1, TS, POOL_WIDTH), lambda b, i: (b, i, 0)),
            pl.BlockSpec((1, H, TS, HEAD_PAD), lambda b, i: (b, 0, i, 0)),
            pl.BlockSpec((1, H, TS, HEAD_PAD), lambda b, i: (b, 0, i, 0)),
            pl.BlockSpec((1, TS, H * V_HEAD), lambda b, i: (b, i, 0)),
        ],
        out_shape=[
            jax.ShapeDtypeStruct((B, S, POOL_WIDTH), BF16),
            jax.ShapeDtypeStruct((B, H, S, HEAD_PAD), BF16),
            jax.ShapeDtypeStruct((B, H, S, HEAD_PAD), BF16),
            jax.ShapeDtypeStruct((B, S, H * V_HEAD), BF16),
        ],
        scratch_shapes=[pltpu.VMEM((TS + 2 * HALO, POOL_WIDTH), F32)],
        compiler_params=_params("parallel", "arbitrary"),
        name="proj",
    )(x, x, x, win_p, w_pool[0].astype(BF16), pool_scale, q_norm_g, wq_p, kv_norm_g, wkv_p, cos_t, sin_t)

    ymla = pl.pallas_call(
        _attn_kernel,
        grid=(B, S // TQ),
        in_specs=[
            pl.BlockSpec((1, H, TQ, HEAD_PAD), lambda b, i: (b, 0, i, 0)),
            pl.BlockSpec((1, H, S, HEAD_PAD), lambda b, i: (b, 0, 0, 0)),
            pl.BlockSpec((1, S, H * V_HEAD), lambda b, i: (b, 0, 0)),
        ],
        out_specs=pl.BlockSpec((1, TQ, H * V_HEAD), lambda b, i: (b, i, 0)),
        out_shape=jax.ShapeDtypeStruct((B, S, H * V_HEAD), BF16),
        compiler_params=_params("parallel", "arbitrary"),
        name="attn",
    )(q, k, v)

    wr_t = w_router[0].T
    wr_hi = wr_t.astype(BF16)
    wr_lo = (wr_t - wr_hi.astype(F32)).astype(BF16)
    wsgu = jnp.concatenate([w_sh_gate[0], w_sh_up[0]], axis=1).astype(BF16)
    n_tm = N // TM
    row_spec = pl.BlockSpec((TM, D), lambda i: (i, 0))
    half_spec = pl.BlockSpec((TM, POOL_WIDTH), lambda i: (i, 0))
    tok_spec = pl.BlockSpec((TOP_K, TM), lambda i: (0, i))
    base, xp, top_e, top_r, top_w, counts = pl.pallas_call(
        _mix_kernel,
        grid=(n_tm,),
        in_specs=[
            row_spec, half_spec, half_spec, _full((D, D)), _full((1, D)), _full((1, D)),
            _full((N_EXPERTS, D)), _full((N_EXPERTS, D)), _full((N_EXPERTS, 1)),
            _full(wsgu.shape), _full((D_SHARED, D)),
        ],
        out_specs=[row_spec, pl.BlockSpec((TM, HALF), lambda i: (i, 0)), tok_spec, tok_spec, tok_spec,
                   _full((N_EXPERTS, 128))],
        out_shape=[
            jax.ShapeDtypeStruct((N, D), F32),
            jax.ShapeDtypeStruct((N, HALF), U32),
            jax.ShapeDtypeStruct((TOP_K, N), I32),
            jax.ShapeDtypeStruct((TOP_K, N), I32),
            jax.ShapeDtypeStruct((TOP_K, N), F32),
            jax.ShapeDtypeStruct((N_EXPERTS, 128), F32),
        ],
        scratch_shapes=[pltpu.VMEM((N_EXPERTS, 128), F32)],
        compiler_params=_params("arbitrary"),
        name="mix",
    )(x.reshape(N, D), ypool.reshape(N, POOL_WIDTH), ymla.reshape(N, H * V_HEAD), w_out[0].astype(BF16),
      ln1_g, ln1_b, wr_hi, wr_lo, router_bias.reshape(N_EXPERTS, 1), wsgu, w_sh_down[0].astype(BF16))

    sizes = counts[:, 0].astype(I32)
    padded = (sizes + BM - 1) // BM * BM
    pad_end = jnp.cumsum(padded)
    pad_start = pad_end - padded
    n_blocks = -(-(N * TOP_K + N_EXPERTS * (BM - 1)) // BM)
    P = n_blocks * BM
    block_e = jnp.minimum(jnp.searchsorted(pad_end, jnp.arange(n_blocks, dtype=I32) * BM, side='right'),
                          N_EXPERTS - 1).astype(I32)

    dblk = min(N, 4096)
    dest = pl.pallas_call(
        _dest_kernel,
        grid_spec=pltpu.PrefetchScalarGridSpec(
            num_scalar_prefetch=1, grid=(N // dblk,),
            in_specs=[pl.BlockSpec((TOP_K, dblk), lambda i, ps: (0, i)),
                      pl.BlockSpec((TOP_K, dblk), lambda i, ps: (0, i))],
            out_specs=pl.BlockSpec((TOP_K, dblk), lambda i, ps: (0, i))),
        out_shape=jax.ShapeDtypeStruct((TOP_K, N), I32),
        compiler_params=_params("arbitrary"),
        name="dest",
    )(pad_start.astype(I32), top_e, top_r)
    dest_t = dest.reshape(TOP_K, n_tm, TM).transpose(1, 0, 2).reshape(n_tm, TOP_K * TM)
    tw_t = top_w.T

    buf = pl.pallas_call(
        _dispatch_kernel,
        grid=(n_tm,),
        in_specs=[_full(dest_t.shape), pl.BlockSpec((TM, HALF), lambda i: (i, 0)),
                  pl.BlockSpec(memory_space=pl.ANY)],
        out_specs=pl.BlockSpec(memory_space=pl.ANY),
        out_shape=jax.ShapeDtypeStruct((P, HALF), U32),
        scratch_shapes=[pltpu.SMEM((TOP_K * TM,), I32), pltpu.SemaphoreType.DMA, pltpu.SemaphoreType.DMA],
        input_output_aliases={2: 0},
        compiler_params=_params("arbitrary"),
        name="dispatch",
    )(dest_t, xp, jnp.zeros((P, HALF), U32))

    wgu = jnp.concatenate([w_gate[0], w_up[0]], axis=2).astype(BF16)
    obuf = pl.pallas_call(
        _ffn_kernel,
        grid_spec=pltpu.PrefetchScalarGridSpec(
            num_scalar_prefetch=1, grid=(n_blocks,),
            in_specs=[pl.BlockSpec((BM, HALF), lambda i, be: (i, 0)),
                      pl.BlockSpec((1, D, 2 * D_EXPERT), lambda i, be: (be[i], 0, 0)),
                      pl.BlockSpec((1, D_EXPERT, D), lambda i, be: (be[i], 0, 0))],
            out_specs=pl.BlockSpec((BM, HALF), lambda i, be: (i, 0))),
        out_shape=jax.ShapeDtypeStruct((P, HALF), U32),
        compiler_params=_params("arbitrary"),
        name="ffn",
    )(block_e, buf, wgu, w_down[0].astype(BF16))

    out = pl.pallas_call(
        _combine_kernel,
        grid=(n_tm,),
        in_specs=[_full(dest_t.shape), row_spec, pl.BlockSpec((TM, TOP_K), lambda i: (i, 0)),
                  _full((1, D)), _full((1, D)), pl.BlockSpec(memory_space=pl.ANY)],
        out_specs=row_spec,
        out_shape=jax.ShapeDtypeStruct((N, D), F32),
        scratch_shapes=[pltpu.VMEM((TOP_K, TM, HALF), U32), pltpu.SMEM((TOP_K * TM,), I32),
                        pltpu.SemaphoreType.DMA, pltpu.SemaphoreType.DMA],
        compiler_params=_params("arbitrary"),
        name="combine",
    )(dest_t, base, tw_t, ln2_g, ln2_b, obuf)
    return out.reshape(B, S, D)
```

```python
import functools

import jax
import jax.numpy as jnp
from jax import lax
from jax.experimental import pallas as pl
from jax.experimental.pallas import tpu as pltpu

F32 = jnp.float32
BF16 = jnp.bfloat16
I32 = jnp.int32
U32 = jnp.uint32

D_MODEL = 1024
POOL_WINDOWS = (2, 4, 8, 16)
POOL_WIDTH = 512
POOL_GROUP = 128
MLA_HEADS = 8
QK_NOPE = 64
QK_ROPE = 32
QK_DIM = QK_NOPE + QK_ROPE
V_HEAD = 64
Q_LORA = 256
KV_LORA = 128
ROPE_THETA = 10000.0
HEAD_PAD = 128
HALO = 8

N_EXPERTS = 64
TOP_K = 8
N_GROUPS = 8
GROUP_SIZE = N_EXPERTS // N_GROUPS
TOPK_GROUPS = 4
D_EXPERT = 256
D_SHARED = 256
ROUTED_SCALE = 2.5

DEPTH = 1
ALPHA = (2 * DEPTH) ** 0.25
LN_EPS = 1e-5
RMS_EPS = 1e-6

VMEM_LIMIT = 56 * 1024 * 1024

TS = 512
TQ = 256
TM = 256
BM = 512
HALF = D_MODEL // 2
SEG = 8
CHUNK = 32
TILE_ROWS = TM * TOP_K + N_EXPERTS * (SEG - 1)
TILE_ROWS += -TILE_ROWS % SEG

NT_DIMS = (((1,), (1,)), ((), ()))


def _pack_halves(lo, hi):
    ulo = lax.bitcast_convert_type(lo.astype(BF16).astype(F32), U32)
    uhi = lax.bitcast_convert_type(hi.astype(BF16).astype(F32), U32)
    return ulo | (uhi >> 16)


def _unpack_halves(p):
    lo = lax.bitcast_convert_type(p & jnp.uint32(0xFFFF0000), F32)
    hi = lax.bitcast_convert_type(p << 16, F32)
    return lo, hi


def _proj_kernel(x_ref, xl_ref, xr_ref, win_ref, wpool_ref, pscale_ref, qg_ref, wq_ref, kvg_ref,
                 wkv_ref, cos_ref, sin_ref, ypool_ref, q_ref, k_ref, v_ref, uext_ref, *, seq_len):
    i = pl.program_id(1)
    n_i = pl.num_programs(1)
    ts = x_ref.shape[1]

    proj = jnp.dot(x_ref[0].astype(BF16), win_ref[...], preferred_element_type=F32)
    u = proj[:, :POOL_WIDTH]

    w_u = win_ref[:, :POOL_WIDTH]
    ul = jnp.dot(xl_ref[0].astype(BF16), w_u, preferred_element_type=F32)
    ur = jnp.dot(xr_ref[0].astype(BF16), w_u, preferred_element_type=F32)
    uext_ref[0:HALO, :] = jnp.where(i > 0, ul, 0.0)
    uext_ref[HALO:HALO + ts, :] = u
    uext_ref[HALO + ts:HALO + ts + HALO, :] = jnp.where(i < n_i - 1, ur, 0.0)

    pos = lax.broadcasted_iota(I32, (ts, 1), 0) + i * ts
    outs = []
    for g, w in enumerate(POOL_WINDOWS):
        left = w // 2
        right = w - 1 - left
        c0, c1 = g * POOL_GROUP, (g + 1) * POOL_GROUP
        acc = uext_ref[HALO - left:HALO - left + ts, c0:c1]
        for j in range(-left + 1, right + 1):
            acc = acc + uext_ref[HALO + j:HALO + j + ts, c0:c1]
        lo = jnp.maximum(pos - left, 0)
        hi = jnp.minimum(pos + right + 1, seq_len)
        pooled = acc / (hi - lo).astype(F32) - u[:, c0:c1]
        outs.append(jnp.dot(pooled.astype(BF16), wpool_ref[g], preferred_element_type=F32))
    ypool_ref[0] = (jnp.concatenate(outs, axis=1) * pscale_ref[...]).astype(BF16)

    cosf = cos_ref[...]
    sinf = sin_ref[...]

    ql = proj[:, POOL_WIDTH:POOL_WIDTH + Q_LORA]
    qn = ql * lax.rsqrt(jnp.mean(ql * ql, axis=-1, keepdims=True) + RMS_EPS) * qg_ref[...]
    qq = jnp.dot(qn.astype(BF16), wq_ref[...], preferred_element_type=F32)
    scale = QK_DIM ** -0.5
    hw = MLA_HEADS * HEAD_PAD
    for h in range(MLA_HEADS):
        qa = qq[:, h * HEAD_PAD:(h + 1) * HEAD_PAD]
        qb = qq[:, hw + h * HEAD_PAD:hw + (h + 1) * HEAD_PAD]
        q_ref[0, h] = ((qa * cosf + qb * sinf) * scale).astype(BF16)

    c_kv = POOL_WIDTH + Q_LORA
    kvl = proj[:, c_kv:c_kv + KV_LORA]
    kvn = kvl * lax.rsqrt(jnp.mean(kvl * kvl, axis=-1, keepdims=True) + RMS_EPS) * kvg_ref[...]
    kv = jnp.dot(kvn.astype(BF16), wkv_ref[...], preferred_element_type=F32)
    c_r = c_kv + KV_LORA
    kr = proj[:, c_r:c_r + HEAD_PAD] * cosf + proj[:, c_r + HEAD_PAD:c_r + 2 * HEAD_PAD] * sinf
    for h in range(MLA_HEADS):
        k_ref[0, h] = (kv[:, h * HEAD_PAD:(h + 1) * HEAD_PAD] + kr).astype(BF16)
    v_ref[0] = kv[:, hw:].astype(BF16)


def _attn_kernel(q_ref, k_ref, v_ref, o_ref):
    tq = q_ref.shape[2]
    quad = 4 * V_HEAD
    lane = lax.broadcasted_iota(I32, (tq, quad), 1)
    for j in range(MLA_HEADS // 4):
        vq = v_ref[0, :, j * quad:(j + 1) * quad]
        acc = jnp.zeros((tq, quad), F32)
        for hh in range(4):
            h = 4 * j + hh
            s = lax.dot_general(q_ref[0, h], k_ref[0, h], NT_DIMS, preferred_element_type=F32)
            m = jnp.max(s, axis=1, keepdims=True)
            p = jnp.exp(s - m)
            l = jnp.sum(p, axis=1, keepdims=True)
            o = jnp.dot(p.astype(BF16), vq, preferred_element_type=F32) / l
            acc = jnp.where((lane >= hh * V_HEAD) & (lane < (hh + 1) * V_HEAD), o, acc)
        o_ref[0, :, j * quad:(j + 1) * quad] = acc.astype(BF16)


def _first_argmax(vals, iota, sentinel):
    m = jnp.max(vals, axis=0, keepdims=True)
    idx = jnp.min(jnp.where(vals == m, iota, sentinel), axis=0, keepdims=True)
    return m, idx


def _mix_kernel(x_ref, yp_ref, ym_ref, wout_ref, g1_ref, b1_ref, wrh_ref, wrl_ref, rb_ref, wsgu_ref,
                wsd_ref, base_ref, hb_ref, lpos_ref, tw_ref, cnt_ref):
    i = pl.program_id(0)
    tm = x_ref.shape[0]

    @pl.when(i == 0)
    def _():
        cnt_ref[...] = jnp.zeros_like(cnt_ref)

    mixed = (jnp.dot(yp_ref[...], wout_ref[:POOL_WIDTH, :], preferred_element_type=F32)
             + jnp.dot(ym_ref[...], wout_ref[POOL_WIDTH:, :], preferred_element_type=F32))
    z = ALPHA * x_ref[...] + mixed
    mu = jnp.mean(z, axis=-1, keepdims=True)
    zc = z - mu
    var = jnp.mean(zc * zc, axis=-1, keepdims=True)
    h = zc * lax.rsqrt(var + LN_EPS) * g1_ref[...] + b1_ref[...]
    hb = h.astype(BF16)

    gu = jnp.dot(hb, wsgu_ref[...], preferred_element_type=F32)
    act = jax.nn.silu(gu[:, :D_SHARED]) * gu[:, D_SHARED:]
    shared = jnp.dot(act.astype(BF16), wsd_ref[...], preferred_element_type=F32)
    base_ref[...] = ALPHA * h + shared
    hb_ref[...] = hb

    h_lo = (h - hb.astype(F32)).astype(BF16)
    logits = (lax.dot_general(wrh_ref[...], hb, NT_DIMS, preferred_element_type=F32)
              + lax.dot_general(wrl_ref[...], hb, NT_DIMS, preferred_element_type=F32)
              + lax.dot_general(wrh_ref[...], h_lo, NT_DIMS, preferred_element_type=F32))
    scores = jax.nn.sigmoid(logits)
    biased = scores + rb_ref[...]

    neg_inf = jnp.float32(-jnp.inf)
    iota_g = lax.broadcasted_iota(I32, (GROUP_SIZE, tm), 0)
    grp_rows = []
    for g in range(N_GROUPS):
        bg = biased[g * GROUP_SIZE:(g + 1) * GROUP_SIZE, :]
        m1, i1 = _first_argmax(bg, iota_g, GROUP_SIZE)
        m2 = jnp.max(jnp.where(iota_g == i1, neg_inf, bg), axis=0, keepdims=True)
        grp_rows.append(m1 + m2)
    grp_score = jnp.concatenate(grp_rows, axis=0)

    iota_n = lax.broadcasted_iota(I32, (N_GROUPS, tm), 0)
    grp_sel = jnp.zeros((N_GROUPS, tm), jnp.bool_)
    for _ in range(TOPK_GROUPS):
        _, gi = _first_argmax(grp_score, iota_n, N_GROUPS)
        hit = iota_n == gi
        grp_sel = grp_sel | hit
        grp_score = jnp.where(hit, neg_inf, grp_score)

    cand = jnp.concatenate(
        [jnp.where(grp_sel[g:g + 1, :], biased[g * GROUP_SIZE:(g + 1) * GROUP_SIZE, :], neg_inf)
         for g in range(N_GROUPS)], axis=0)

    iota_e = lax.broadcasted_iota(I32, (N_EXPERTS, tm), 0)
    sel = jnp.zeros((N_EXPERTS, tm), jnp.bool_)
    e_rows, w_rows = [], []
    for _ in range(TOP_K):
        _, ei = _first_argmax(cand, iota_e, N_EXPERTS)
        hit = iota_e == ei
        e_rows.append(ei)
        w_rows.append(jnp.sum(jnp.where(hit, scores, 0.0), axis=0, keepdims=True))
        sel = sel | hit
        cand = jnp.where(hit, neg_inf, cand)
    top_s = jnp.concatenate(w_rows, axis=0)
    top_w = top_s / jnp.sum(top_s, axis=0, keepdims=True) * ROUTED_SCALE

    sel_f = sel.astype(F32)
    r_i = lax.broadcasted_iota(I32, (tm, tm), 0)
    c_i = lax.broadcasted_iota(I32, (tm, tm), 1)
    before = (r_i < c_i).astype(BF16)
    rank = jnp.dot(sel_f.astype(BF16), before, preferred_element_type=F32)
    count = jnp.sum(sel_f, axis=1, keepdims=True)
    granules = jnp.ceil(count * (1.0 / SEG))
    e_r = lax.broadcasted_iota(I32, (N_EXPERTS, N_EXPERTS), 0)
    e_c = lax.broadcasted_iota(I32, (N_EXPERTS, N_EXPERTS), 1)
    earlier = (e_c < e_r).astype(BF16)
    g_b = jnp.broadcast_to(granules, (N_EXPERTS, 128)).astype(BF16)
    seg_start = jnp.dot(earlier, g_b, preferred_element_type=F32)[:, 0:1] * SEG
    pos = seg_start + rank
    p_rows = [jnp.sum(jnp.where(iota_e == e_rows[kk], pos, 0.0), axis=0, keepdims=True)
              for kk in range(TOP_K)]
    lpos_ref[...] = jnp.concatenate(p_rows, axis=0).astype(I32)
    tw_ref[...] = top_w
    lane_t = lax.broadcasted_iota(I32, cnt_ref.shape, 1)
    cnt_ref[...] = jnp.where(lane_t == i, granules * SEG, cnt_ref[...])


def _for_each_segment_copy(ls_ref, c8_ref, gs_ref, tile, make_copy, act):
    def seg(e, carry):
        j = tile * N_EXPERTS + e
        ls, n, gs = ls_ref[j], c8_ref[j], gs_ref[j]
        n_big = lax.shift_right_logical(n, CHUNK.bit_length() - 1)
        n_small = lax.shift_right_logical(n - n_big * CHUNK, SEG.bit_length() - 1)

        def big(q, c):
            off = q * CHUNK
            act(make_copy(pl.multiple_of(ls + off, SEG), pl.multiple_of(gs + off, SEG), CHUNK))
            return c

        def small(q, c):
            off = n_big * CHUNK + q * SEG
            act(make_copy(pl.multiple_of(ls + off, SEG), pl.multiple_of(gs + off, SEG), SEG))
            return c

        lax.fori_loop(0, n_big, big, 0)
        lax.fori_loop(0, n_small, small, 0)
        return carry

    lax.fori_loop(0, N_EXPERTS, seg, 0)


def _dispatch_kernel(ls_ref, c8_ref, gs_ref, lpos_ref, hb_ref, zeros_ref, buf_ref, sorted_ref, sem):
    del zeros_ref
    i = pl.program_id(0)
    tm = hb_ref.shape[0]
    lpos = lpos_ref[...]
    iota_p = lax.broadcasted_iota(I32, (TILE_ROWS, tm), 0)
    onehot = jnp.zeros((TILE_ROWS, tm), F32)
    for kk in range(TOP_K):
        onehot = jnp.where(iota_p == lpos[kk:kk + 1, :], 1.0, onehot)
    s = jnp.dot(onehot.astype(BF16), hb_ref[...], preferred_element_type=F32)
    sorted_ref[...] = (lax.bitcast_convert_type(s[:, :HALF], U32)
                       | (lax.bitcast_convert_type(s[:, HALF:], U32) >> 16))

    def make_copy(lrow, grow, rows):
        return pltpu.make_async_copy(sorted_ref.at[pl.ds(lrow, rows)], buf_ref.at[pl.ds(grow, rows)], sem)

    _for_each_segment_copy(ls_ref, c8_ref, gs_ref, i, make_copy, lambda cp: cp.start())
    _for_each_segment_copy(ls_ref, c8_ref, gs_ref, i, make_copy, lambda cp: cp.wait())


def _ffn_kernel(be_ref, nu_ref, x_ref, wgu_ref, wd_ref, o_ref):
    del be_ref
    i = pl.program_id(0)

    @pl.when(i < nu_ref[0])
    def _():
        lo, hi = _unpack_halves(x_ref[...])
        x = jnp.concatenate([lo.astype(BF16), hi.astype(BF16)], axis=1)
        gu = jnp.dot(x, wgu_ref[0], preferred_element_type=F32)
        act = jax.nn.silu(gu[:, :D_EXPERT]) * gu[:, D_EXPERT:]
        y = jnp.dot(act.astype(BF16), wd_ref[0], preferred_element_type=F32)
        o_ref[...] = _pack_halves(y[:, :HALF], y[:, HALF:])

    @pl.when(i >= nu_ref[0])
    def _():
        o_ref[...] = jnp.zeros_like(o_ref)


def _combine_kernel(ls_ref, c8_ref, gs_ref, lpos_ref, tw_ref, base_ref, g2_ref, b2_ref, obuf_ref, out_ref,
                    sorted_ref, sem):
    i = pl.program_id(0)
    tm = base_ref.shape[0]

    @pl.when(i == 0)
    def _():
        sorted_ref[...] = jnp.zeros_like(sorted_ref)

    def make_copy(lrow, grow, rows):
        return pltpu.make_async_copy(obuf_ref.at[pl.ds(grow, rows)], sorted_ref.at[pl.ds(lrow, rows)], sem)

    _for_each_segment_copy(ls_ref, c8_ref, gs_ref, i, make_copy, lambda cp: cp.start())
    _for_each_segment_copy(ls_ref, c8_ref, gs_ref, i, make_copy, lambda cp: cp.wait())

    iota_l = lax.broadcasted_iota(I32, (tm, TILE_ROWS), 1)
    weights = jnp.zeros((tm, TILE_ROWS), F32)
    for kk in range(TOP_K):
        weights = jnp.where(iota_l == lpos_ref[:, kk:kk + 1], tw_ref[:, kk:kk + 1], weights)
    wb = weights.astype(BF16)
    lo, hi = _unpack_halves(sorted_ref[...])
    routed = jnp.concatenate([jnp.dot(wb, lo.astype(BF16), preferred_element_type=F32),
                              jnp.dot(wb, hi.astype(BF16), preferred_element_type=F32)], axis=1)
    z = base_ref[...] + routed
    mu = jnp.mean(z, axis=-1, keepdims=True)
    zc = z - mu
    var = jnp.mean(zc * zc, axis=-1, keepdims=True)
    out_ref[...] = zc * lax.rsqrt(var + LN_EPS) * g2_ref[...] + b2_ref[...]


def _full(shape):
    nd = len(shape)
    return pl.BlockSpec(shape, lambda *_: (0,) * nd)


def _params(*sem):
    return pltpu.CompilerParams(dimension_semantics=sem, vmem_limit_bytes=VMEM_LIMIT)


def kernel(x, w_in, w_pool, pool_scale, q_norm_g, w_q_up, kv_norm_g, w_kv_up, w_out, ln1_g, ln1_b,
           w_router, router_bias, w_gate, w_up, w_down, w_sh_gate, w_sh_up, w_sh_down, ln2_g, ln2_b):
    B, S, D = x.shape
    assert D == D_MODEL and S % TS == 0 and S % TQ == 0 and (B * S) % TM == 0 and TS % HALO == 0
    N = B * S
    H = MLA_HEADS
    hw = H * HEAD_PAD

    wi = w_in[0]
    c_r = POOL_WIDTH + Q_LORA + KV_LORA
    rope = wi[:, c_r:c_r + QK_ROPE]
    half = QK_ROPE // 2
    zc64 = jnp.zeros((D, QK_NOPE), F32)
    zc32 = jnp.zeros((D, HEAD_PAD - QK_DIM), F32)
    rope_a = jnp.concatenate([zc64, rope, zc32], axis=1)
    rope_b = jnp.concatenate([zc64, -rope[:, half:], rope[:, :half], zc32], axis=1)
    win_p = jnp.concatenate([wi[:, :c_r], rope_a, rope_b], axis=1).astype(BF16)

    wq = w_q_up[0].reshape(Q_LORA, H, QK_DIM)
    zq = jnp.zeros((Q_LORA, H, HEAD_PAD - QK_DIM), F32)
    wq_a = jnp.concatenate([wq, zq], axis=2).reshape(Q_LORA, hw)
    wq_b = jnp.concatenate([jnp.zeros((Q_LORA, H, QK_NOPE), F32), -wq[:, :, QK_NOPE + half:],
                            wq[:, :, QK_NOPE:QK_NOPE + half], zq], axis=2).reshape(Q_LORA, hw)
    wq_p = jnp.concatenate([wq_a, wq_b], axis=1).astype(BF16)

    wkv = w_kv_up[0].reshape(KV_LORA, H, QK_NOPE + V_HEAD)
    wk = jnp.concatenate([wkv[:, :, :QK_NOPE], jnp.zeros((KV_LORA, H, HEAD_PAD - QK_NOPE), F32)],
                         axis=2).reshape(KV_LORA, hw)
    wv = wkv[:, :, QK_NOPE:].reshape(KV_LORA, H * V_HEAD)
    wkv_p = jnp.concatenate([wk, wv], axis=1).astype(BF16)

    pos = jnp.arange(S, dtype=F32)
    inv_freq = ROPE_THETA ** (-jnp.arange(0, QK_ROPE, 2, dtype=F32) / QK_ROPE)
    ang = pos[:, None] * inv_freq[None, :]
    cosv, sinv = jnp.cos(ang), jnp.sin(ang)
    cos_t = jnp.concatenate([jnp.ones((S, QK_NOPE), F32), cosv, cosv,
                             jnp.zeros((S, HEAD_PAD - QK_DIM), F32)], axis=1)
    sin_t = jnp.concatenate([jnp.zeros((S, QK_NOPE), F32), sinv, sinv,
                             jnp.zeros((S, HEAD_PAD - QK_DIM), F32)], axis=1)

    n_ts = S // TS
    hb = TS // HALO
    ypool, q, k, v = pl.pallas_call(
        functools.partial(_proj_kernel, seq_len=S),
        grid=(B, n_ts),
        in_specs=[
            pl.BlockSpec((1, TS, D), lambda b, i: (b, i, 0)),
            pl.BlockSpec((1, HALO, D), lambda b, i: (b, jnp.maximum(i * hb - 1, 0), 0)),
            pl.BlockSpec((1, HALO, D), lambda b, i: (b, jnp.minimum((i + 1) * hb, S // HALO - 1), 0)),
            _full(win_p.shape), _full((len(POOL_WINDOWS), POOL_GROUP, POOL_GROUP)),
            _full((1, POOL_WIDTH)), _full((1, Q_LORA)), _full(wq_p.shape), _full((1, KV_LORA)),
            _full(wkv_p.shape),
            pl.BlockSpec((TS, HEAD_PAD), lambda b, i: (i, 0)),
            pl.BlockSpec((TS, HEAD_PAD), lambda b, i: (i, 0)),
        ],
        out_specs=[
            pl.BlockSpec((1, TS, POOL_WIDTH), lambda b, i: (b, i, 0)),
            pl.BlockSpec((1, H, TS, HEAD_PAD), lambda b, i: (b, 0, i, 0)),
            pl.BlockSpec((1, H, TS, HEAD_PAD), lambda b, i: (b, 0, i, 0)),
            pl.BlockSpec((1, TS, H * V_HEAD), lambda b, i: (b, i, 0)),
        ],
        out_shape=[
            jax.ShapeDtypeStruct((B, S, POOL_WIDTH), BF16),
            jax.ShapeDtypeStruct((B, H, S, HEAD_PAD), BF16),
            jax.ShapeDtypeStruct((B, H, S, HEAD_PAD), BF16),
            jax.ShapeDtypeStruct((B, S, H * V_HEAD), BF16),
        ],
        scratch_shapes=[pltpu.VMEM((TS + 2 * HALO, POOL_WIDTH), F32)],
        compiler_params=_params("parallel", "arbitrary"),
        name="proj",
    )(x, x, x, win_p, w_pool[0].astype(BF16), pool_scale, q_norm_g, wq_p, kv_norm_g, wkv_p, cos_t, sin_t)

    ymla = pl.pallas_call(
        _attn_kernel,
        grid=(B, S // TQ),
        in_specs=[
            pl.BlockSpec((1, H, TQ, HEAD_PAD), lambda b, i: (b, 0, i, 0)),
            pl.BlockSpec((1, H, S, HEAD_PAD), lambda b, i: (b, 0, 0, 0)),
            pl.BlockSpec((1, S, H * V_HEAD), lambda b, i: (b, 0, 0)),
        ],
        out_specs=pl.BlockSpec((1, TQ, H * V_HEAD), lambda b, i: (b, i, 0)),
        out_shape=jax.ShapeDtypeStruct((B, S, H * V_HEAD), BF16),
        compiler_params=_params("parallel", "arbitrary"),
        name="attn",
    )(q, k, v)

    wr_t = w_router[0].T
    wr_hi = wr_t.astype(BF16)
    wr_lo = (wr_t - wr_hi.astype(F32)).astype(BF16)
    wsgu = jnp.concatenate([w_sh_gate[0], w_sh_up[0]], axis=1).astype(BF16)
    n_tm = N // TM
    row_spec = pl.BlockSpec((TM, D), lambda i: (i, 0))
    half_spec = pl.BlockSpec((TM, POOL_WIDTH), lambda i: (i, 0))
    tok_spec = pl.BlockSpec((TOP_K, TM), lambda i: (0, i))
    base, hb, lpos, top_w, tile_cnt = pl.pallas_call(
        _mix_kernel,
        grid=(n_tm,),
        in_specs=[
            row_spec, half_spec, half_spec, _full((D, D)), _full((1, D)), _full((1, D)),
            _full((N_EXPERTS, D)), _full((N_EXPERTS, D)), _full((N_EXPERTS, 1)),
            _full(wsgu.shape), _full((D_SHARED, D)),
        ],
        out_specs=[row_spec, row_spec, tok_spec, tok_spec, _full((N_EXPERTS, n_tm))],
        out_shape=[
            jax.ShapeDtypeStruct((N, D), F32),
            jax.ShapeDtypeStruct((N, D), BF16),
            jax.ShapeDtypeStruct((TOP_K, N), I32),
            jax.ShapeDtypeStruct((TOP_K, N), F32),
            jax.ShapeDtypeStruct((N_EXPERTS, n_tm), F32),
        ],
        compiler_params=_params("arbitrary"),
        name="mix",
    )(x.reshape(N, D), ypool.reshape(N, POOL_WIDTH), ymla.reshape(N, H * V_HEAD), w_out[0].astype(BF16),
      ln1_g, ln1_b, wr_hi, wr_lo, router_bias.reshape(N_EXPERTS, 1), wsgu, w_sh_down[0].astype(BF16))

    c8 = tile_cnt.T.astype(I32)
    l_start = jnp.cumsum(c8, axis=1) - c8
    expert_rows = jnp.sum(c8, axis=0)
    padded = (expert_rows + BM - 1) // BM * BM
    pad_end = jnp.cumsum(padded)
    g_start = (pad_end - padded)[None, :] + jnp.cumsum(c8, axis=0) - c8
    n_blocks = -(-(N * TOP_K + n_tm * N_EXPERTS * (SEG - 1) + N_EXPERTS * (BM - 1)) // BM)
    P = n_blocks * BM
    blk_row = jnp.arange(n_blocks, dtype=I32) * BM
    block_e = jnp.minimum(jnp.sum((pad_end[None, :] <= blk_row[:, None]).astype(I32), axis=1), N_EXPERTS - 1)
    n_used = (pad_end[-1] // BM).astype(I32).reshape(1)
    seg_tables = (l_start.reshape(-1), c8.reshape(-1), g_start.reshape(-1).astype(I32))

    buf = pl.pallas_call(
        _dispatch_kernel,
        grid_spec=pltpu.PrefetchScalarGridSpec(
            num_scalar_prefetch=3, grid=(n_tm,),
            in_specs=[pl.BlockSpec((TOP_K, TM), lambda i, *_: (0, i)),
                      pl.BlockSpec((TM, D), lambda i, *_: (i, 0)),
                      pl.BlockSpec(memory_space=pl.ANY)],
            out_specs=pl.BlockSpec(memory_space=pl.ANY),
            scratch_shapes=[pltpu.VMEM((TILE_ROWS, HALF), U32), pltpu.SemaphoreType.DMA]),
        out_shape=jax.ShapeDtypeStruct((P, HALF), U32),
        input_output_aliases={5: 0},
        compiler_params=_params("arbitrary"),
        name="dispatch",
    )(*seg_tables, lpos, hb, jnp.zeros((P, HALF), U32))

    wgu = jnp.concatenate([w_gate[0], w_up[0]], axis=2).astype(BF16)

    def used(i, nu):
        return jnp.minimum(i, nu[0] - 1)

    obuf = pl.pallas_call(
        _ffn_kernel,
        grid_spec=pltpu.PrefetchScalarGridSpec(
            num_scalar_prefetch=2, grid=(n_blocks,),
            in_specs=[pl.BlockSpec((BM, HALF), lambda i, be, nu: (used(i, nu), 0)),
                      pl.BlockSpec((1, D, 2 * D_EXPERT), lambda i, be, nu: (be[used(i, nu)], 0, 0)),
                      pl.BlockSpec((1, D_EXPERT, D), lambda i, be, nu: (be[used(i, nu)], 0, 0))],
            out_specs=pl.BlockSpec((BM, HALF), lambda i, be, nu: (i, 0))),
        out_shape=jax.ShapeDtypeStruct((P, HALF), U32),
        compiler_params=_params("arbitrary"),
        name="ffn",
    )(block_e, n_used, buf, wgu, w_down[0].astype(BF16))

    out = pl.pallas_call(
        _combine_kernel,
        grid_spec=pltpu.PrefetchScalarGridSpec(
            num_scalar_prefetch=3, grid=(n_tm,),
            in_specs=[pl.BlockSpec((TM, TOP_K), lambda i, *_: (i, 0)),
                      pl.BlockSpec((TM, TOP_K), lambda i, *_: (i, 0)),
                      pl.BlockSpec((TM, D), lambda i, *_: (i, 0)),
                      pl.BlockSpec((1, D), lambda i, *_: (0, 0)), pl.BlockSpec((1, D), lambda i, *_: (0, 0)),
                      pl.BlockSpec(memory_space=pl.ANY)],
            out_specs=pl.BlockSpec((TM, D), lambda i, *_: (i, 0)),
            scratch_shapes=[pltpu.VMEM((TILE_ROWS, HALF), U32), pltpu.SemaphoreType.DMA]),
        out_shape=jax.ShapeDtypeStruct((N, D), F32),
        compiler_params=_params("arbitrary"),
        name="combine",
    )(*seg_tables, lpos.T, top_w.T, base, ln2_g, ln2_b, obuf)
    return out.reshape(B, S, D)
```

```python
import functools

import jax
import jax.numpy as jnp
from jax import lax
from jax.experimental import pallas as pl
from jax.experimental.pallas import tpu as pltpu

F32 = jnp.float32
BF16 = jnp.bfloat16
I32 = jnp.int32
U32 = jnp.uint32

D_MODEL = 1024
POOL_WINDOWS = (2, 4, 8, 16)
POOL_WIDTH = 512
POOL_GROUP = 128
MLA_HEADS = 8
QK_NOPE = 64
QK_ROPE = 32
QK_DIM = QK_NOPE + QK_ROPE
V_HEAD = 64
Q_LORA = 256
KV_LORA = 128
ROPE_THETA = 10000.0
HEAD_PAD = 128
HALO = 8

N_EXPERTS = 64
TOP_K = 8
N_GROUPS = 8
GROUP_SIZE = N_EXPERTS // N_GROUPS
TOPK_GROUPS = 4
D_EXPERT = 256
D_SHARED = 256
ROUTED_SCALE = 2.5

DEPTH = 1
ALPHA = (2 * DEPTH) ** 0.25
LN_EPS = 1e-5
RMS_EPS = 1e-6

VMEM_LIMIT = 56 * 1024 * 1024

TS = 512
TQ = 256
TM = 256
BM = 512
HALF = D_MODEL // 2
SEG = 8
CHUNK = 32
TILE_ROWS = TM * TOP_K + N_EXPERTS * (SEG - 1)
TILE_ROWS += -TILE_ROWS % SEG

NT_DIMS = (((1,), (1,)), ((), ()))


def _pack_halves(lo, hi):
    ulo = lax.bitcast_convert_type(lo.astype(BF16).astype(F32), U32)
    uhi = lax.bitcast_convert_type(hi.astype(BF16).astype(F32), U32)
    return ulo | (uhi >> 16)


def _unpack_halves(p):
    lo = lax.bitcast_convert_type(p & jnp.uint32(0xFFFF0000), F32)
    hi = lax.bitcast_convert_type(p << 16, F32)
    return lo, hi


def _proj_kernel(x_ref, xl_ref, xr_ref, win_ref, wpool_ref, pscale_ref, qg_ref, wq_ref, kvg_ref,
                 wkv_ref, cos_ref, sin_ref, ypool_ref, q_ref, k_ref, v_ref, uext_ref, *, seq_len):
    i = pl.program_id(1)
    n_i = pl.num_programs(1)
    ts = x_ref.shape[1]

    proj = jnp.dot(x_ref[0].astype(BF16), win_ref[...], preferred_element_type=F32)
    u = proj[:, :POOL_WIDTH]

    w_u = win_ref[:, :POOL_WIDTH]
    ul = jnp.dot(xl_ref[0].astype(BF16), w_u, preferred_element_type=F32)
    ur = jnp.dot(xr_ref[0].astype(BF16), w_u, preferred_element_type=F32)
    uext_ref[0:HALO, :] = jnp.where(i > 0, ul, 0.0)
    uext_ref[HALO:HALO + ts, :] = u
    uext_ref[HALO + ts:HALO + ts + HALO, :] = jnp.where(i < n_i - 1, ur, 0.0)

    pos = lax.broadcasted_iota(I32, (ts, 1), 0) + i * ts
    outs = []
    for g, w in enumerate(POOL_WINDOWS):
        left = w // 2
        right = w - 1 - left
        c0, c1 = g * POOL_GROUP, (g + 1) * POOL_GROUP
        acc = uext_ref[HALO - left:HALO - left + ts, c0:c1]
        for j in range(-left + 1, right + 1):
            acc = acc + uext_ref[HALO + j:HALO + j + ts, c0:c1]
        lo = jnp.maximum(pos - left, 0)
        hi = jnp.minimum(pos + right + 1, seq_len)
        pooled = acc / (hi - lo).astype(F32) - u[:, c0:c1]
        outs.append(jnp.dot(pooled.astype(BF16), wpool_ref[g], preferred_element_type=F32))
    ypool_ref[0] = (jnp.concatenate(outs, axis=1) * pscale_ref[...]).astype(BF16)

    cosf = cos_ref[...]
    sinf = sin_ref[...]

    ql = proj[:, POOL_WIDTH:POOL_WIDTH + Q_LORA]
    qn = ql * lax.rsqrt(jnp.mean(ql * ql, axis=-1, keepdims=True) + RMS_EPS) * qg_ref[...]
    qq = jnp.dot(qn.astype(BF16), wq_ref[...], preferred_element_type=F32)
    scale = QK_DIM ** -0.5
    hw = MLA_HEADS * HEAD_PAD
    for h in range(MLA_HEADS):
        qa = qq[:, h * HEAD_PAD:(h + 1) * HEAD_PAD]
        qb = qq[:, hw + h * HEAD_PAD:hw + (h + 1) * HEAD_PAD]
        q_ref[0, h] = ((qa * cosf + qb * sinf) * scale).astype(BF16)

    c_kv = POOL_WIDTH + Q_LORA
    kvl = proj[:, c_kv:c_kv + KV_LORA]
    kvn = kvl * lax.rsqrt(jnp.mean(kvl * kvl, axis=-1, keepdims=True) + RMS_EPS) * kvg_ref[...]
    kv = jnp.dot(kvn.astype(BF16), wkv_ref[...], preferred_element_type=F32)
    c_r = c_kv + KV_LORA
    kr = proj[:, c_r:c_r + HEAD_PAD] * cosf + proj[:, c_r + HEAD_PAD:c_r + 2 * HEAD_PAD] * sinf
    for h in range(MLA_HEADS):
        k_ref[0, h] = (kv[:, h * HEAD_PAD:(h + 1) * HEAD_PAD] + kr).astype(BF16)
    v_ref[0] = kv[:, hw:].astype(BF16)


def _attn_kernel(q_ref, k_ref, v_ref, o_ref):
    tq = q_ref.shape[2]
    quad = 4 * V_HEAD
    lane = lax.broadcasted_iota(I32, (tq, quad), 1)
    for j in range(MLA_HEADS // 4):
        vq = v_ref[0, :, j * quad:(j + 1) * quad]
        acc = jnp.zeros((tq, quad), F32)
        for hh in range(4):
            h = 4 * j + hh
            s = lax.dot_general(q_ref[0, h], k_ref[0, h], NT_DIMS, preferred_element_type=F32)
            m = jnp.max(s, axis=1, keepdims=True)
            p = jnp.exp(s - m)
            l = jnp.sum(p, axis=1, keepdims=True)
            o = jnp.dot(p.astype(BF16), vq, preferred_element_type=F32) / l
            acc = jnp.where((lane >= hh * V_HEAD) & (lane < (hh + 1) * V_HEAD), o, acc)
        o_ref[0, :, j * quad:(j + 1) * quad] = acc.astype(BF16)


def _first_argmax(vals, iota, sentinel):
    m = jnp.max(vals, axis=0, keepdims=True)
    idx = jnp.min(jnp.where(vals == m, iota, sentinel), axis=0, keepdims=True)
    return m, idx


def _mix_kernel(x_ref, yp_ref, ym_ref, wout_ref, g1_ref, b1_ref, wrh_ref, wrl_ref, rb_ref, wsgu_ref,
                wsd_ref, base_ref, hb_ref, lpos_ref, tw_ref, cnt_ref):
    i = pl.program_id(0)
    tm = x_ref.shape[0]

    @pl.when(i == 0)
    def _():
        cnt_ref[...] = jnp.zeros_like(cnt_ref)

    mixed = (jnp.dot(yp_ref[...], wout_ref[:POOL_WIDTH, :], preferred_element_type=F32)
             + jnp.dot(ym_ref[...], wout_ref[POOL_WIDTH:, :], preferred_element_type=F32))
    z = ALPHA * x_ref[...] + mixed
    mu = jnp.mean(z, axis=-1, keepdims=True)
    zc = z - mu
    var = jnp.mean(zc * zc, axis=-1, keepdims=True)
    h = zc * lax.rsqrt(var + LN_EPS) * g1_ref[...] + b1_ref[...]
    hb = h.astype(BF16)

    gu = jnp.dot(hb, wsgu_ref[...], preferred_element_type=F32)
    act = jax.nn.silu(gu[:, :D_SHARED]) * gu[:, D_SHARED:]
    shared = jnp.dot(act.astype(BF16), wsd_ref[...], preferred_element_type=F32)
    base_ref[...] = ALPHA * h + shared
    hb_ref[...] = hb

    h_lo = (h - hb.astype(F32)).astype(BF16)
    logits = (lax.dot_general(wrh_ref[...], hb, NT_DIMS, preferred_element_type=F32)
              + lax.dot_general(wrl_ref[...], hb, NT_DIMS, preferred_element_type=F32)
              + lax.dot_general(wrh_ref[...], h_lo, NT_DIMS, preferred_element_type=F32))
    scores = jax.nn.sigmoid(logits)
    biased = scores + rb_ref[...]

    neg_inf = jnp.float32(-jnp.inf)
    iota_g = lax.broadcasted_iota(I32, (GROUP_SIZE, tm), 0)
    grp_rows = []
    for g in range(N_GROUPS):
        bg = biased[g * GROUP_SIZE:(g + 1) * GROUP_SIZE, :]
        m1, i1 = _first_argmax(bg, iota_g, GROUP_SIZE)
        m2 = jnp.max(jnp.where(iota_g == i1, neg_inf, bg), axis=0, keepdims=True)
        grp_rows.append(m1 + m2)
    grp_score = jnp.concatenate(grp_rows, axis=0)

    iota_n = lax.broadcasted_iota(I32, (N_GROUPS, tm), 0)
    grp_sel = jnp.zeros((N_GROUPS, tm), jnp.bool_)
    for _ in range(TOPK_GROUPS):
        _, gi = _first_argmax(grp_score, iota_n, N_GROUPS)
        hit = iota_n == gi
        grp_sel = grp_sel | hit
        grp_score = jnp.where(hit, neg_inf, grp_score)

    cand = jnp.concatenate(
        [jnp.where(grp_sel[g:g + 1, :], biased[g * GROUP_SIZE:(g + 1) * GROUP_SIZE, :], neg_inf)
         for g in range(N_GROUPS)], axis=0)

    iota_e = lax.broadcasted_iota(I32, (N_EXPERTS, tm), 0)
    sel = jnp.zeros((N_EXPERTS, tm), jnp.bool_)
    e_rows, w_rows = [], []
    for _ in range(TOP_K):
        _, ei = _first_argmax(cand, iota_e, N_EXPERTS)
        hit = iota_e == ei
        e_rows.append(ei)
        w_rows.append(jnp.sum(jnp.where(hit, scores, 0.0), axis=0, keepdims=True))
        sel = sel | hit
        cand = jnp.where(hit, neg_inf, cand)
    top_s = jnp.concatenate(w_rows, axis=0)
    top_w = top_s / jnp.sum(top_s, axis=0, keepdims=True) * ROUTED_SCALE

    sel_f = sel.astype(F32)
    r_i = lax.broadcasted_iota(I32, (tm, tm), 0)
    c_i = lax.broadcasted_iota(I32, (tm, tm), 1)
    before = (r_i < c_i).astype(BF16)
    rank = jnp.dot(sel_f.astype(BF16), before, preferred_element_type=F32)
    count = jnp.sum(sel_f, axis=1, keepdims=True)
    granules = jnp.ceil(count * (1.0 / SEG))
    e_r = lax.broadcasted_iota(I32, (N_EXPERTS, N_EXPERTS), 0)
    e_c = lax.broadcasted_iota(I32, (N_EXPERTS, N_EXPERTS), 1)
    earlier = (e_c < e_r).astype(BF16)
    g_b = jnp.broadcast_to(granules, (N_EXPERTS, 128)).astype(BF16)
    seg_start = jnp.dot(earlier, g_b, preferred_element_type=F32)[:, 0:1] * SEG
    pos = seg_start + rank
    p_rows = [jnp.sum(jnp.where(iota_e == e_rows[kk], pos, 0.0), axis=0, keepdims=True)
              for kk in range(TOP_K)]
    lpos_ref[...] = jnp.concatenate(p_rows, axis=0).astype(I32)
    tw_ref[...] = top_w
    lane_t = lax.broadcasted_iota(I32, cnt_ref.shape, 1)
    cnt_ref[...] = jnp.where(lane_t == i, granules * SEG, cnt_ref[...])


def _for_each_segment_copy(ls_ref, c8_ref, gs_ref, tile, make_copy, act):
    def seg(e, carry):
        j = tile * N_EXPERTS + e
        ls, n, gs = ls_ref[j], c8_ref[j], gs_ref[j]
        n_big = lax.shift_right_logical(n, CHUNK.bit_length() - 1)
        n_small = lax.shift_right_logical(n - n_big * CHUNK, SEG.bit_length() - 1)

        def big(q, c):
            off = q * CHUNK
            act(make_copy(pl.multiple_of(ls + off, SEG), pl.multiple_of(gs + off, SEG), CHUNK))
            return c

        def small(q, c):
            off = n_big * CHUNK + q * SEG
            act(make_copy(pl.multiple_of(ls + off, SEG), pl.multiple_of(gs + off, SEG), SEG))
            return c

        lax.fori_loop(0, n_big, big, 0)
        lax.fori_loop(0, n_small, small, 0)
        return carry

    lax.fori_loop(0, N_EXPERTS, seg, 0)


def _dispatch_kernel(ls_ref, c8_ref, gs_ref, fs_ref, fn_ref, lpos_ref, hb_ref, buf_ref, sorted_ref,
                     zero_ref, sem, zsem):
    i = pl.program_id(0)
    n_i = pl.num_programs(0)
    tm = hb_ref.shape[0]
    slot = lax.rem(i, 2)
    lpos = lpos_ref[...]
    iota_p = lax.broadcasted_iota(I32, (TILE_ROWS, tm), 0)
    onehot = jnp.zeros((TILE_ROWS, tm), F32)
    for kk in range(TOP_K):
        onehot = jnp.where(iota_p == lpos[kk:kk + 1, :], 1.0, onehot)
    s = jnp.dot(onehot.astype(BF16), hb_ref[...], preferred_element_type=F32)
    sorted_ref[slot] = (lax.bitcast_convert_type(s[:, :HALF], U32)
                        | (lax.bitcast_convert_type(s[:, HALF:], U32) >> 16))

    def copies_from(sl):
        def make_copy(lrow, grow, rows):
            return pltpu.make_async_copy(sorted_ref.at[sl, pl.ds(lrow, rows)], buf_ref.at[pl.ds(grow, rows)],
                                         sem.at[sl])
        return make_copy

    _for_each_segment_copy(ls_ref, c8_ref, gs_ref, i, copies_from(slot), lambda cp: cp.start())

    @pl.when(i > 0)
    def _():
        _for_each_segment_copy(ls_ref, c8_ref, gs_ref, i - 1, copies_from(1 - slot), lambda cp: cp.wait())

    @pl.when(i == n_i - 1)
    def _():
        _for_each_segment_copy(ls_ref, c8_ref, gs_ref, i, copies_from(slot), lambda cp: cp.wait())
        zero_ref[...] = jnp.zeros_like(zero_ref)

        def zero_copy(grow, rows):
            return pltpu.make_async_copy(zero_ref.at[pl.ds(0, rows)], buf_ref.at[pl.ds(grow, rows)], zsem)

        def fill(act):
            def gap(e, carry):
                gs, n = fs_ref[e], fn_ref[e]
                n_big = lax.shift_right_logical(n, CHUNK.bit_length() - 1)
                n_small = lax.shift_right_logical(n - n_big * CHUNK, SEG.bit_length() - 1)

                def big(q, c):
                    act(zero_copy(pl.multiple_of(gs + q * CHUNK, SEG), CHUNK))
                    return c

                def small(q, c):
                    act(zero_copy(pl.multiple_of(gs + n_big * CHUNK + q * SEG, SEG), SEG))
                    return c

                lax.fori_loop(0, n_big, big, 0)
                lax.fori_loop(0, n_small, small, 0)
                return carry

            lax.fori_loop(0, fs_ref.shape[0], gap, 0)

        fill(lambda cp: cp.start())
        fill(lambda cp: cp.wait())


def _ffn_kernel(be_ref, nu_ref, x_ref, wg_ref, wu_ref, wd_ref, o_ref, wgu_s, wd_s):
    i = pl.program_id(0)
    n_used = nu_ref[0]

    @pl.when((i == 0) | ((i < n_used) & (be_ref[i] != be_ref[jnp.maximum(i - 1, 0)])))
    def _():
        wgu_s[:, :D_EXPERT] = wg_ref[0].astype(BF16)
        wgu_s[:, D_EXPERT:] = wu_ref[0].astype(BF16)
        wd_s[...] = wd_ref[0].astype(BF16)

    @pl.when(i < n_used)
    def _():
        lo, hi = _unpack_halves(x_ref[...])
        x = jnp.concatenate([lo.astype(BF16), hi.astype(BF16)], axis=1)
        gu = jnp.dot(x, wgu_s[...], preferred_element_type=F32)
        act = jax.nn.silu(gu[:, :D_EXPERT]) * gu[:, D_EXPERT:]
        y = jnp.dot(act.astype(BF16), wd_s[...], preferred_element_type=F32)
        o_ref[...] = _pack_halves(y[:, :HALF], y[:, HALF:])

    @pl.when(i >= n_used)
    def _():
        o_ref[...] = jnp.zeros_like(o_ref)


def _combine_kernel(ls_ref, c8_ref, gs_ref, lpos_ref, tw_ref, base_ref, g2_ref, b2_ref, obuf_ref, out_ref,
                    sorted_ref, sem):
    i = pl.program_id(0)
    n_i = pl.num_programs(0)
    tm = base_ref.shape[0]
    slot = lax.rem(i, 2)

    def copies_into(sl):
        def make_copy(lrow, grow, rows):
            return pltpu.make_async_copy(obuf_ref.at[pl.ds(grow, rows)], sorted_ref.at[sl, pl.ds(lrow, rows)],
                                         sem.at[sl])
        return make_copy

    @pl.when(i == 0)
    def _():
        sorted_ref[...] = jnp.zeros_like(sorted_ref)
        _for_each_segment_copy(ls_ref, c8_ref, gs_ref, i, copies_into(slot), lambda cp: cp.start())

    @pl.when(i + 1 < n_i)
    def _():
        _for_each_segment_copy(ls_ref, c8_ref, gs_ref, i + 1, copies_into(1 - slot), lambda cp: cp.start())

    iota_l = lax.broadcasted_iota(I32, (tm, TILE_ROWS), 1)
    weights = jnp.zeros((tm, TILE_ROWS), F32)
    for kk in range(TOP_K):
        weights = jnp.where(iota_l == lpos_ref[:, kk:kk + 1], tw_ref[:, kk:kk + 1], weights)
    wb = weights.astype(BF16)

    _for_each_segment_copy(ls_ref, c8_ref, gs_ref, i, copies_into(slot), lambda cp: cp.wait())
    lo, hi = _unpack_halves(sorted_ref[slot])
    routed = jnp.concatenate([jnp.dot(wb, lo.astype(BF16), preferred_element_type=F32),
                              jnp.dot(wb, hi.astype(BF16), preferred_element_type=F32)], axis=1)
    z = base_ref[...] + routed
    mu = jnp.mean(z, axis=-1, keepdims=True)
    zc = z - mu
    var = jnp.mean(zc * zc, axis=-1, keepdims=True)
    out_ref[...] = zc * lax.rsqrt(var + LN_EPS) * g2_ref[...] + b2_ref[...]


def _full(shape):
    nd = len(shape)
    return pl.BlockSpec(shape, lambda *_: (0,) * nd)


def _params(*sem):
    return pltpu.CompilerParams(dimension_semantics=sem, vmem_limit_bytes=VMEM_LIMIT)


def kernel(x, w_in, w_pool, pool_scale, q_norm_g, w_q_up, kv_norm_g, w_kv_up, w_out, ln1_g, ln1_b,
           w_router, router_bias, w_gate, w_up, w_down, w_sh_gate, w_sh_up, w_sh_down, ln2_g, ln2_b):
    B, S, D = x.shape
    assert D == D_MODEL and S % TS == 0 and S % TQ == 0 and (B * S) % TM == 0 and TS % HALO == 0
    N = B * S
    H = MLA_HEADS
    hw = H * HEAD_PAD

    wi = w_in[0]
    c_r = POOL_WIDTH + Q_LORA + KV_LORA
    rope = wi[:, c_r:c_r + QK_ROPE]
    half = QK_ROPE // 2
    zc64 = jnp.zeros((D, QK_NOPE), F32)
    zc32 = jnp.zeros((D, HEAD_PAD - QK_DIM), F32)
    rope_a = jnp.concatenate([zc64, rope, zc32], axis=1)
    rope_b = jnp.concatenate([zc64, -rope[:, half:], rope[:, :half], zc32], axis=1)
    win_p = jnp.concatenate([wi[:, :c_r], rope_a, rope_b], axis=1).astype(BF16)

    wq = w_q_up[0].reshape(Q_LORA, H, QK_DIM)
    zq = jnp.zeros((Q_LORA, H, HEAD_PAD - QK_DIM), F32)
    wq_a = jnp.concatenate([wq, zq], axis=2).reshape(Q_LORA, hw)
    wq_b = jnp.concatenate([jnp.zeros((Q_LORA, H, QK_NOPE), F32), -wq[:, :, QK_NOPE + half:],
                            wq[:, :, QK_NOPE:QK_NOPE + half], zq], axis=2).reshape(Q_LORA, hw)
    wq_p = jnp.concatenate([wq_a, wq_b], axis=1).astype(BF16)

    wkv = w_kv_up[0].reshape(KV_LORA, H, QK_NOPE + V_HEAD)
    wk = jnp.concatenate([wkv[:, :, :QK_NOPE], jnp.zeros((KV_LORA, H, HEAD_PAD - QK_NOPE), F32)],
                         axis=2).reshape(KV_LORA, hw)
    wv = wkv[:, :, QK_NOPE:].reshape(KV_LORA, H * V_HEAD)
    wkv_p = jnp.concatenate([wk, wv], axis=1).astype(BF16)

    pos = jnp.arange(S, dtype=F32)
    inv_freq = ROPE_THETA ** (-jnp.arange(0, QK_ROPE, 2, dtype=F32) / QK_ROPE)
    ang = pos[:, None] * inv_freq[None, :]
    cosv, sinv = jnp.cos(ang), jnp.sin(ang)
    cos_t = jnp.concatenate([jnp.ones((S, QK_NOPE), F32), cosv, cosv,
                             jnp.zeros((S, HEAD_PAD - QK_DIM), F32)], axis=1)
    sin_t = jnp.concatenate([jnp.zeros((S, QK_NOPE), F32), sinv, sinv,
                             jnp.zeros((S, HEAD_PAD - QK_DIM), F32)], axis=1)

    n_ts = S // TS
    hb = TS // HALO
    ypool, q, k, v = pl.pallas_call(
        functools.partial(_proj_kernel, seq_len=S),
        grid=(B, n_ts),
        in_specs=[
            pl.BlockSpec((1, TS, D), lambda b, i: (b, i, 0)),
            pl.BlockSpec((1, HALO, D), lambda b, i: (b, jnp.maximum(i * hb - 1, 0), 0)),
            pl.BlockSpec((1, HALO, D), lambda b, i: (b, jnp.minimum((i + 1) * hb, S // HALO - 1), 0)),
            _full(win_p.shape), _full((len(POOL_WINDOWS), POOL_GROUP, POOL_GROUP)),
            _full((1, POOL_WIDTH)), _full((1, Q_LORA)), _full(wq_p.shape), _full((1, KV_LORA)),
            _full(wkv_p.shape),
            pl.BlockSpec((TS, HEAD_PAD), lambda b, i: (i, 0)),
            pl.BlockSpec((TS, HEAD_PAD), lambda b, i: (i, 0)),
        ],
        out_specs=[
            pl.BlockSpec((1, TS, POOL_WIDTH), lambda b, i: (b, i, 0)),
            pl.BlockSpec((1, H, TS, HEAD_PAD), lambda b, i: (b, 0, i, 0)),
            pl.BlockSpec((1, H, TS, HEAD_PAD), lambda b, i: (b, 0, i, 0)),
            pl.BlockSpec((1, TS, H * V_HEAD), lambda b, i: (b, i, 0)),
        ],
        out_shape=[
            jax.ShapeDtypeStruct((B, S, POOL_WIDTH), BF16),
            jax.ShapeDtypeStruct((B, H, S, HEAD_PAD), BF16),
            jax.ShapeDtypeStruct((B, H, S, HEAD_PAD), BF16),
            jax.ShapeDtypeStruct((B, S, H * V_HEAD), BF16),
        ],
        scratch_shapes=[pltpu.VMEM((TS + 2 * HALO, POOL_WIDTH), F32)],
        compiler_params=_params("parallel", "arbitrary"),
        name="proj",
    )(x, x, x, win_p, w_pool[0].astype(BF16), pool_scale, q_norm_g, wq_p, kv_norm_g, wkv_p, cos_t, sin_t)

    ymla = pl.pallas_call(
        _attn_kernel,
        grid=(B, S // TQ),
        in_specs=[
            pl.BlockSpec((1, H, TQ, HEAD_PAD), lambda b, i: (b, 0, i, 0)),
            pl.BlockSpec((1, H, S, HEAD_PAD), lambda b, i: (b, 0, 0, 0)),
            pl.BlockSpec((1, S, H * V_HEAD), lambda b, i: (b, 0, 0)),
        ],
        out_specs=pl.BlockSpec((1, TQ, H * V_HEAD), lambda b, i: (b, i, 0)),
        out_shape=jax.ShapeDtypeStruct((B, S, H * V_HEAD), BF16),
        compiler_params=_params("parallel", "arbitrary"),
        name="attn",
    )(q, k, v)

    wr_t = w_router[0].T
    wr_hi = wr_t.astype(BF16)
    wr_lo = (wr_t - wr_hi.astype(F32)).astype(BF16)
    wsgu = jnp.concatenate([w_sh_gate[0], w_sh_up[0]], axis=1).astype(BF16)
    n_tm = N // TM
    row_spec = pl.BlockSpec((TM, D), lambda i: (i, 0))
    half_spec = pl.BlockSpec((TM, POOL_WIDTH), lambda i: (i, 0))
    tok_spec = pl.BlockSpec((TOP_K, TM), lambda i: (0, i))
    base, hb, lpos, top_w, tile_cnt = pl.pallas_call(
        _mix_kernel,
        grid=(n_tm,),
        in_specs=[
            row_spec, half_spec, half_spec, _full((D, D)), _full((1, D)), _full((1, D)),
            _full((N_EXPERTS, D)), _full((N_EXPERTS, D)), _full((N_EXPERTS, 1)),
            _full(wsgu.shape), _full((D_SHARED, D)),
        ],
        out_specs=[row_spec, row_spec, tok_spec, tok_spec, _full((N_EXPERTS, n_tm))],
        out_shape=[
            jax.ShapeDtypeStruct((N, D), F32),
            jax.ShapeDtypeStruct((N, D), BF16),
            jax.ShapeDtypeStruct((TOP_K, N), I32),
            jax.ShapeDtypeStruct((TOP_K, N), F32),
            jax.ShapeDtypeStruct((N_EXPERTS, n_tm), F32),
        ],
        compiler_params=_params("arbitrary"),
        name="mix",
    )(x.reshape(N, D), ypool.reshape(N, POOL_WIDTH), ymla.reshape(N, H * V_HEAD), w_out[0].astype(BF16),
      ln1_g, ln1_b, wr_hi, wr_lo, router_bias.reshape(N_EXPERTS, 1), wsgu, w_sh_down[0].astype(BF16))

    c8 = tile_cnt.T.astype(I32)
    l_start = jnp.cumsum(c8, axis=1) - c8
    expert_rows = jnp.sum(c8, axis=0)
    padded = (expert_rows + BM - 1) // BM * BM
    pad_end = jnp.cumsum(padded)
    g_start = (pad_end - padded)[None, :] + jnp.cumsum(c8, axis=0) - c8
    n_blocks = -(-(N * TOP_K + n_tm * N_EXPERTS * (SEG - 1) + N_EXPERTS * (BM - 1)) // BM)
    P = n_blocks * BM
    blk_row = jnp.arange(n_blocks, dtype=I32) * BM
    block_e = jnp.minimum(jnp.sum((pad_end[None, :] <= blk_row[:, None]).astype(I32), axis=1), N_EXPERTS - 1)
    n_used = (pad_end[-1] // BM).astype(I32).reshape(1)
    seg_tables = (l_start.reshape(-1), c8.reshape(-1), g_start.reshape(-1).astype(I32))
    fill_start = jnp.concatenate([pad_end - padded + expert_rows, pad_end[-1:]]).astype(I32)
    fill_rows = jnp.concatenate([padded - expert_rows, P - pad_end[-1:]]).astype(I32)

    buf = pl.pallas_call(
        _dispatch_kernel,
        grid_spec=pltpu.PrefetchScalarGridSpec(
            num_scalar_prefetch=5, grid=(n_tm,),
            in_specs=[pl.BlockSpec((TOP_K, TM), lambda i, *_: (0, i)),
                      pl.BlockSpec((TM, D), lambda i, *_: (i, 0))],
            out_specs=pl.BlockSpec(memory_space=pl.ANY),
            scratch_shapes=[pltpu.VMEM((2, TILE_ROWS, HALF), U32), pltpu.VMEM((CHUNK, HALF), U32),
                            pltpu.SemaphoreType.DMA((2,)), pltpu.SemaphoreType.DMA]),
        out_shape=jax.ShapeDtypeStruct((P, HALF), U32),
        compiler_params=_params("arbitrary"),
        name="dispatch",
    )(*seg_tables, fill_start, fill_rows, lpos, hb)

    def used(i, nu):
        return jnp.minimum(i, nu[0] - 1)

    obuf = pl.pallas_call(
        _ffn_kernel,
        grid_spec=pltpu.PrefetchScalarGridSpec(
            num_scalar_prefetch=2, grid=(n_blocks,),
            in_specs=[pl.BlockSpec((BM, HALF), lambda i, be, nu: (used(i, nu), 0)),
                      pl.BlockSpec((1, D, D_EXPERT), lambda i, be, nu: (be[used(i, nu)], 0, 0)),
                      pl.BlockSpec((1, D, D_EXPERT), lambda i, be, nu: (be[used(i, nu)], 0, 0)),
                      pl.BlockSpec((1, D_EXPERT, D), lambda i, be, nu: (be[used(i, nu)], 0, 0))],
            out_specs=pl.BlockSpec((BM, HALF), lambda i, be, nu: (i, 0)),
            scratch_shapes=[pltpu.VMEM((D, 2 * D_EXPERT), BF16), pltpu.VMEM((D_EXPERT, D), BF16)]),
        out_shape=jax.ShapeDtypeStruct((P, HALF), U32),
        compiler_params=_params("arbitrary"),
        name="ffn",
    )(block_e, n_used, buf, w_gate[0], w_up[0], w_down[0])

    out = pl.pallas_call(
        _combine_kernel,
        grid_spec=pltpu.PrefetchScalarGridSpec(
            num_scalar_prefetch=3, grid=(n_tm,),
            in_specs=[pl.BlockSpec((TM, TOP_K), lambda i, *_: (i, 0)),
                      pl.BlockSpec((TM, TOP_K), lambda i, *_: (i, 0)),
                      pl.BlockSpec((TM, D), lambda i, *_: (i, 0)),
                      pl.BlockSpec((1, D), lambda i, *_: (0, 0)), pl.BlockSpec((1, D), lambda i, *_: (0, 0)),
                      pl.BlockSpec(memory_space=pl.ANY)],
            out_specs=pl.BlockSpec((TM, D), lambda i, *_: (i, 0)),
            scratch_shapes=[pltpu.VMEM((2, TILE_ROWS, HALF), U32), pltpu.SemaphoreType.DMA((2,))]),
        out_shape=jax.ShapeDtypeStruct((N, D), F32),
        compiler_params=_params("arbitrary"),
        name="combine",
    )(*seg_tables, lpos.T, top_w.T, base, ln2_g, ln2_b, obuf)
    return out.reshape(B, S, D)
```

```python
import functools

import jax
import jax.numpy as jnp
from jax import lax
from jax.experimental import pallas as pl
from jax.experimental.pallas import tpu as pltpu

F32 = jnp.float32
BF16 = jnp.bfloat16
I32 = jnp.int32
U32 = jnp.uint32

D_MODEL = 1024
POOL_WINDOWS = (2, 4, 8, 16)
POOL_WIDTH = 512
POOL_GROUP = 128
MLA_HEADS = 8
QK_NOPE = 64
QK_ROPE = 32
QK_DIM = QK_NOPE + QK_ROPE
V_HEAD = 64
Q_LORA = 256
KV_LORA = 128
ROPE_THETA = 10000.0
HEAD_PAD = 128
HALO = 8

N_EXPERTS = 64
TOP_K = 8
N_GROUPS = 8
GROUP_SIZE = N_EXPERTS // N_GROUPS
TOPK_GROUPS = 4
D_EXPERT = 256
D_SHARED = 256
ROUTED_SCALE = 2.5

DEPTH = 1
ALPHA = (2 * DEPTH) ** 0.25
LN_EPS = 1e-5
RMS_EPS = 1e-6
LOG2_E = 1.4426950408889634

VMEM_LIMIT = 56 * 1024 * 1024

TS = 512
TQ = 256
TM = 256
BM = 512
HALF = D_MODEL // 2
SEG = 8
CHUNK = 32
TILE_ROWS = TM * TOP_K + N_EXPERTS * (SEG - 1)
TILE_ROWS += -TILE_ROWS % SEG
MAX_BIG = TILE_ROWS // CHUNK
MAX_SMALL = N_EXPERTS * (CHUNK // SEG - 1)
LOCAL_BITS = (TILE_ROWS // SEG - 1).bit_length()

NT_DIMS = (((1,), (1,)), ((), ()))


def _pack_halves(lo, hi):
    ulo = lax.bitcast_convert_type(lo.astype(BF16).astype(F32), U32)
    uhi = lax.bitcast_convert_type(hi.astype(BF16).astype(F32), U32)
    return ulo | (uhi >> 16)


def _unpack_halves(p):
    lo = lax.bitcast_convert_type(p & jnp.uint32(0xFFFF0000), F32)
    hi = lax.bitcast_convert_type(p << 16, F32)
    return lo, hi


def _proj_kernel(x_ref, xl_ref, xr_ref, win_ref, wpool_ref, pscale_ref, qg_ref, wq_ref, kvg_ref,
                 wkv_ref, cos_ref, sin_ref, ypool_ref, q_ref, k_ref, v_ref, uext_ref, *, seq_len):
    i = pl.program_id(1)
    n_i = pl.num_programs(1)
    ts = x_ref.shape[1]

    proj = jnp.dot(x_ref[0].astype(BF16), win_ref[...], preferred_element_type=F32)
    u = proj[:, :POOL_WIDTH]

    w_u = win_ref[:, :POOL_WIDTH]
    ul = jnp.dot(xl_ref[0].astype(BF16), w_u, preferred_element_type=F32)
    ur = jnp.dot(xr_ref[0].astype(BF16), w_u, preferred_element_type=F32)
    uext_ref[0:HALO, :] = jnp.where(i > 0, ul, 0.0)
    uext_ref[HALO:HALO + ts, :] = u
    uext_ref[HALO + ts:HALO + ts + HALO, :] = jnp.where(i < n_i - 1, ur, 0.0)

    pos = lax.broadcasted_iota(I32, (ts, 1), 0) + i * ts
    outs = []
    for g, w in enumerate(POOL_WINDOWS):
        left = w // 2
        right = w - 1 - left
        c0, c1 = g * POOL_GROUP, (g + 1) * POOL_GROUP
        acc = uext_ref[HALO - left:HALO - left + ts, c0:c1]
        for j in range(-left + 1, right + 1):
            acc = acc + uext_ref[HALO + j:HALO + j + ts, c0:c1]
        lo = jnp.maximum(pos - left, 0)
        hi = jnp.minimum(pos + right + 1, seq_len)
        pooled = acc / (hi - lo).astype(F32) - u[:, c0:c1]
        outs.append(jnp.dot(pooled.astype(BF16), wpool_ref[g], preferred_element_type=F32))
    ypool_ref[0] = (jnp.concatenate(outs, axis=1) * pscale_ref[...]).astype(BF16)

    cosf = cos_ref[...]
    sinf = sin_ref[...]

    ql = proj[:, POOL_WIDTH:POOL_WIDTH + Q_LORA]
    qn = ql * lax.rsqrt(jnp.mean(ql * ql, axis=-1, keepdims=True) + RMS_EPS) * qg_ref[...]
    qq = jnp.dot(qn.astype(BF16), wq_ref[...], preferred_element_type=F32)
    scale = QK_DIM ** -0.5 * LOG2_E
    hw = MLA_HEADS * HEAD_PAD
    for h in range(MLA_HEADS):
        qa = qq[:, h * HEAD_PAD:(h + 1) * HEAD_PAD]
        qb = qq[:, hw + h * HEAD_PAD:hw + (h + 1) * HEAD_PAD]
        q_ref[0, h] = ((qa * cosf + qb * sinf) * scale).astype(BF16)

    c_kv = POOL_WIDTH + Q_LORA
    kvl = proj[:, c_kv:c_kv + KV_LORA]
    kvn = kvl * lax.rsqrt(jnp.mean(kvl * kvl, axis=-1, keepdims=True) + RMS_EPS) * kvg_ref[...]
    kv = jnp.dot(kvn.astype(BF16), wkv_ref[...], preferred_element_type=F32)
    c_r = c_kv + KV_LORA
    kr = proj[:, c_r:c_r + HEAD_PAD] * cosf + proj[:, c_r + HEAD_PAD:c_r + 2 * HEAD_PAD] * sinf
    for h in range(MLA_HEADS):
        k_ref[0, h] = (kv[:, h * HEAD_PAD:(h + 1) * HEAD_PAD] + kr).astype(BF16)
    v_ref[0] = kv[:, hw:].astype(BF16)


def _attn_kernel(q_ref, k_ref, v_ref, o_ref):
    tq = q_ref.shape[2]
    quad = 4 * V_HEAD
    lane = lax.broadcasted_iota(I32, (tq, quad), 1)
    for j in range(MLA_HEADS // 4):
        vq = v_ref[0, :, j * quad:(j + 1) * quad]
        acc = jnp.zeros((tq, quad), F32)
        for hh in range(4):
            h = 4 * j + hh
            s = lax.dot_general(q_ref[0, h], k_ref[0, h], NT_DIMS, preferred_element_type=F32)
            m = jnp.max(s, axis=1, keepdims=True)
            p = jnp.exp2(s - m)
            l = jnp.sum(p, axis=1, keepdims=True)
            o = jnp.dot(p.astype(BF16), vq, preferred_element_type=F32) / l
            acc = jnp.where((lane >= hh * V_HEAD) & (lane < (hh + 1) * V_HEAD), o, acc)
        o_ref[0, :, j * quad:(j + 1) * quad] = acc.astype(BF16)


def _first_argmax(vals, iota, sentinel):
    m = jnp.max(vals, axis=0, keepdims=True)
    idx = jnp.min(jnp.where(vals == m, iota, sentinel), axis=0, keepdims=True)
    return m, idx


def _mix_kernel(x_ref, yp_ref, ym_ref, wout_ref, g1_ref, b1_ref, wrh_ref, wrl_ref, rb_ref, wsgu_ref,
                wsd_ref, base_ref, hb_ref, lpos_ref, tw_ref, cnt_ref):
    i = pl.program_id(0)
    tm = x_ref.shape[0]

    @pl.when(i == 0)
    def _():
        cnt_ref[...] = jnp.zeros_like(cnt_ref)

    mixed = (jnp.dot(yp_ref[...], wout_ref[:POOL_WIDTH, :], preferred_element_type=F32)
             + jnp.dot(ym_ref[...], wout_ref[POOL_WIDTH:, :], preferred_element_type=F32))
    z = ALPHA * x_ref[...] + mixed
    mu = jnp.mean(z, axis=-1, keepdims=True)
    zc = z - mu
    var = jnp.mean(zc * zc, axis=-1, keepdims=True)
    h = zc * lax.rsqrt(var + LN_EPS) * g1_ref[...] + b1_ref[...]
    hb = h.astype(BF16)

    gu = jnp.dot(hb, wsgu_ref[...], preferred_element_type=F32)
    act = jax.nn.silu(gu[:, :D_SHARED]) * gu[:, D_SHARED:]
    shared = jnp.dot(act.astype(BF16), wsd_ref[...], preferred_element_type=F32)
    base_ref[...] = ALPHA * h + shared
    hb_ref[...] = hb

    h_lo = (h - hb.astype(F32)).astype(BF16)
    logits = (lax.dot_general(wrh_ref[...], hb, NT_DIMS, preferred_element_type=F32)
              + lax.dot_general(wrl_ref[...], hb, NT_DIMS, preferred_element_type=F32)
              + lax.dot_general(wrh_ref[...], h_lo, NT_DIMS, preferred_element_type=F32))
    scores = jax.nn.sigmoid(logits)
    biased = scores + rb_ref[...]

    neg_inf = jnp.float32(-jnp.inf)
    iota_g = lax.broadcasted_iota(I32, (GROUP_SIZE, tm), 0)
    grp_rows = []
    for g in range(N_GROUPS):
        bg = biased[g * GROUP_SIZE:(g + 1) * GROUP_SIZE, :]
        m1, i1 = _first_argmax(bg, iota_g, GROUP_SIZE)
        m2 = jnp.max(jnp.where(iota_g == i1, neg_inf, bg), axis=0, keepdims=True)
        grp_rows.append(m1 + m2)
    grp_score = jnp.concatenate(grp_rows, axis=0)

    iota_n = lax.broadcasted_iota(I32, (N_GROUPS, tm), 0)
    grp_sel = jnp.zeros((N_GROUPS, tm), jnp.bool_)
    for _ in range(TOPK_GROUPS):
        _, gi = _first_argmax(grp_score, iota_n, N_GROUPS)
        hit = iota_n == gi
        grp_sel = grp_sel | hit
        grp_score = jnp.where(hit, neg_inf, grp_score)

    cand = jnp.concatenate(
        [jnp.where(grp_sel[g:g + 1, :], biased[g * GROUP_SIZE:(g + 1) * GROUP_SIZE, :], neg_inf)
         for g in range(N_GROUPS)], axis=0)

    iota_e = lax.broadcasted_iota(I32, (N_EXPERTS, tm), 0)
    sel = jnp.zeros((N_EXPERTS, tm), jnp.bool_)
    e_rows, w_rows = [], []
    for _ in range(TOP_K):
        _, ei = _first_argmax(cand, iota_e, N_EXPERTS)
        hit = iota_e == ei
        e_rows.append(ei)
        w_rows.append(jnp.sum(jnp.where(hit, scores, 0.0), axis=0, keepdims=True))
        sel = sel | hit
        cand = jnp.where(hit, neg_inf, cand)
    top_s = jnp.concatenate(w_rows, axis=0)
    top_w = top_s / jnp.sum(top_s, axis=0, keepdims=True) * ROUTED_SCALE

    sel_f = sel.astype(F32)
    r_i = lax.broadcasted_iota(I32, (tm, tm), 0)
    c_i = lax.broadcasted_iota(I32, (tm, tm), 1)
    before = (r_i < c_i).astype(BF16)
    rank = jnp.dot(sel_f.astype(BF16), before, preferred_element_type=F32)
    count = jnp.sum(sel_f, axis=1, keepdims=True)
    granules = jnp.ceil(count * (1.0 / SEG))
    e_r = lax.broadcasted_iota(I32, (N_EXPERTS, N_EXPERTS), 0)
    e_c = lax.broadcasted_iota(I32, (N_EXPERTS, N_EXPERTS), 1)
    earlier = (e_c < e_r).astype(BF16)
    g_b = jnp.broadcast_to(granules, (N_EXPERTS, 128)).astype(BF16)
    seg_start = jnp.dot(earlier, g_b, preferred_element_type=F32)[:, 0:1] * SEG
    pos = seg_start + rank
    p_rows = [jnp.sum(jnp.where(iota_e == e_rows[kk], pos, 0.0), axis=0, keepdims=True)
              for kk in range(TOP_K)]
    lpos_ref[...] = jnp.concatenate(p_rows, axis=0).astype(I32)
    tw_ref[...] = top_w
    lane_t = lax.broadcasted_iota(I32, cnt_ref.shape, 1)
    cnt_ref[...] = jnp.where(lane_t == i, granules * SEG, cnt_ref[...])


def _for_each_tile_copy(tabs, tile, make_copy, act):
    big_ref, small_ref, nbig_ref, nsmall_ref = tabs
    for tab_ref, n_ref, cap, rows in ((big_ref, nbig_ref, MAX_BIG, CHUNK), (small_ref, nsmall_ref, MAX_SMALL, SEG)):
        def body(q, c, tab_ref=tab_ref, cap=cap, rows=rows):
            p = tab_ref[tile * cap + q]
            lrow = pl.multiple_of((p & (2 ** LOCAL_BITS - 1)) * SEG, SEG)
            grow = pl.multiple_of(lax.shift_right_logical(p, LOCAL_BITS) * SEG, SEG)
            act(make_copy(lrow, grow, rows))
            return c

        lax.fori_loop(0, n_ref[tile], body, 0)


def _dispatch_kernel(big_ref, small_ref, nbig_ref, nsmall_ref, fs_ref, fn_ref, lpos_ref, hb_ref, buf_ref,
                     sorted_ref, zero_ref, sem, zsem):
    tabs = (big_ref, small_ref, nbig_ref, nsmall_ref)
    i = pl.program_id(0)
    n_i = pl.num_programs(0)
    tm = hb_ref.shape[0]
    slot = lax.rem(i, 2)
    lpos = lpos_ref[...]
    iota_p = lax.broadcasted_iota(I32, (TILE_ROWS, tm), 0)
    onehot = jnp.zeros((TILE_ROWS, tm), F32)
    for kk in range(TOP_K):
        onehot = jnp.where(iota_p == lpos[kk:kk + 1, :], 1.0, onehot)
    s = jnp.dot(onehot.astype(BF16), hb_ref[...], preferred_element_type=F32)
    sorted_ref[slot] = (lax.bitcast_convert_type(s[:, :HALF], U32)
                        | (lax.bitcast_convert_type(s[:, HALF:], U32) >> 16))

    def copies_from(sl):
        def make_copy(lrow, grow, rows):
            return pltpu.make_async_copy(sorted_ref.at[sl, pl.ds(lrow, rows)], buf_ref.at[pl.ds(grow, rows)],
                                         sem.at[sl])
        return make_copy

    _for_each_tile_copy(tabs, i, copies_from(slot), lambda cp: cp.start())

    @pl.when(i > 0)
    def _():
        _for_each_tile_copy(tabs, i - 1, copies_from(1 - slot), lambda cp: cp.wait())

    @pl.when(i == n_i - 1)
    def _():
        _for_each_tile_copy(tabs, i, copies_from(slot), lambda cp: cp.wait())
        zero_ref[...] = jnp.zeros_like(zero_ref)

        def zero_copy(grow, rows):
            return pltpu.make_async_copy(zero_ref.at[pl.ds(0, rows)], buf_ref.at[pl.ds(grow, rows)], zsem)

        def fill(act):
            def gap(e, carry):
                gs, n = fs_ref[e], fn_ref[e]
                n_big = lax.shift_right_logical(n, CHUNK.bit_length() - 1)
                n_small = lax.shift_right_logical(n - n_big * CHUNK, SEG.bit_length() - 1)

                def big(q, c):
                    act(zero_copy(pl.multiple_of(gs + q * CHUNK, SEG), CHUNK))
                    return c

                def small(q, c):
                    act(zero_copy(pl.multiple_of(gs + n_big * CHUNK + q * SEG, SEG), SEG))
                    return c

                lax.fori_loop(0, n_big, big, 0)
                lax.fori_loop(0, n_small, small, 0)
                return carry

            lax.fori_loop(0, fs_ref.shape[0], gap, 0)

        fill(lambda cp: cp.start())
        fill(lambda cp: cp.wait())


def _ffn_kernel(be_ref, nu_ref, x_ref, wg_ref, wu_ref, wd_ref, o_ref, wgu_s, wd_s):
    i = pl.program_id(0)
    n_used = nu_ref[0]

    @pl.when((i == 0) | ((i < n_used) & (be_ref[i] != be_ref[jnp.maximum(i - 1, 0)])))
    def _():
        wgu_s[:, :D_EXPERT] = wg_ref[0].astype(BF16)
        wgu_s[:, D_EXPERT:] = wu_ref[0].astype(BF16)
        wd_s[...] = wd_ref[0].astype(BF16)

    @pl.when(i < n_used)
    def _():
        lo, hi = _unpack_halves(x_ref[...])
        x = jnp.concatenate([lo.astype(BF16), hi.astype(BF16)], axis=1)
        gu = jnp.dot(x, wgu_s[...], preferred_element_type=F32)
        act = jax.nn.silu(gu[:, :D_EXPERT]) * gu[:, D_EXPERT:]
        y = jnp.dot(act.astype(BF16), wd_s[...], preferred_element_type=F32)
        o_ref[...] = _pack_halves(y[:, :HALF], y[:, HALF:])

    @pl.when(i >= n_used)
    def _():
        o_ref[...] = jnp.zeros_like(o_ref)


def _combine_kernel(big_ref, small_ref, nbig_ref, nsmall_ref, lpos_ref, tw_ref, base_ref, g2_ref, b2_ref,
                    obuf_ref, out_ref, sorted_ref, sem):
    tabs = (big_ref, small_ref, nbig_ref, nsmall_ref)
    i = pl.program_id(0)
    n_i = pl.num_programs(0)
    tm = base_ref.shape[0]
    slot = lax.rem(i, 2)

    def copies_into(sl):
        def make_copy(lrow, grow, rows):
            return pltpu.make_async_copy(obuf_ref.at[pl.ds(grow, rows)], sorted_ref.at[sl, pl.ds(lrow, rows)],
                                         sem.at[sl])
        return make_copy

    @pl.when(i == 0)
    def _():
        sorted_ref[...] = jnp.zeros_like(sorted_ref)
        _for_each_tile_copy(tabs, i, copies_into(slot), lambda cp: cp.start())

    @pl.when(i + 1 < n_i)
    def _():
        _for_each_tile_copy(tabs, i + 1, copies_into(1 - slot), lambda cp: cp.start())

    iota_l = lax.broadcasted_iota(I32, (tm, TILE_ROWS), 1)
    weights = jnp.zeros((tm, TILE_ROWS), F32)
    for kk in range(TOP_K):
        weights = jnp.where(iota_l == lpos_ref[:, kk:kk + 1], tw_ref[:, kk:kk + 1], weights)
    wb = weights.astype(BF16)

    _for_each_tile_copy(tabs, i, copies_into(slot), lambda cp: cp.wait())
    lo, hi = _unpack_halves(sorted_ref[slot])
    routed = jnp.concatenate([jnp.dot(wb, lo.astype(BF16), preferred_element_type=F32),
                              jnp.dot(wb, hi.astype(BF16), preferred_element_type=F32)], axis=1)
    z = base_ref[...] + routed
    mu = jnp.mean(z, axis=-1, keepdims=True)
    zc = z - mu
    var = jnp.mean(zc * zc, axis=-1, keepdims=True)
    out_ref[...] = zc * lax.rsqrt(var + LN_EPS) * g2_ref[...] + b2_ref[...]


def _full(shape):
    nd = len(shape)
    return pl.BlockSpec(shape, lambda *_: (0,) * nd)


def _chunk_table(count, first_row, rows, cap, l_start, g_start):
    end = jnp.cumsum(count, axis=1)
    q = jnp.arange(cap, dtype=I32)
    expert = jnp.minimum(jnp.sum((end[:, None, :] <= q[None, :, None]).astype(I32), axis=2), N_EXPERTS - 1)
    hit = expert[:, :, None] == jnp.arange(N_EXPERTS, dtype=I32)[None, None, :]

    def pick(a):
        return jnp.sum(jnp.where(hit, a[:, None, :], 0), axis=2)

    off = pick(first_row) + (q[None, :] - pick(end - count)) * rows
    entry = ((pick(g_start) + off) // SEG) * 2 ** LOCAL_BITS + (pick(l_start) + off) // SEG
    return entry.reshape(-1).astype(I32), end[:, -1].astype(I32)


def _params(*sem):
    return pltpu.CompilerParams(dimension_semantics=sem, vmem_limit_bytes=VMEM_LIMIT)


def kernel(x, w_in, w_pool, pool_scale, q_norm_g, w_q_up, kv_norm_g, w_kv_up, w_out, ln1_g, ln1_b,
           w_router, router_bias, w_gate, w_up, w_down, w_sh_gate, w_sh_up, w_sh_down, ln2_g, ln2_b):
    B, S, D = x.shape
    assert D == D_MODEL and S % TS == 0 and S % TQ == 0 and (B * S) % TM == 0 and TS % HALO == 0
    N = B * S
    H = MLA_HEADS
    hw = H * HEAD_PAD

    wi = w_in[0]
    c_r = POOL_WIDTH + Q_LORA + KV_LORA
    rope = wi[:, c_r:c_r + QK_ROPE]
    half = QK_ROPE // 2
    zc64 = jnp.zeros((D, QK_NOPE), F32)
    zc32 = jnp.zeros((D, HEAD_PAD - QK_DIM), F32)
    rope_a = jnp.concatenate([zc64, rope, zc32], axis=1)
    rope_b = jnp.concatenate([zc64, -rope[:, half:], rope[:, :half], zc32], axis=1)
    win_p = jnp.concatenate([wi[:, :c_r], rope_a, rope_b], axis=1).astype(BF16)

    wq = w_q_up[0].reshape(Q_LORA, H, QK_DIM)
    zq = jnp.zeros((Q_LORA, H, HEAD_PAD - QK_DIM), F32)
    wq_a = jnp.concatenate([wq, zq], axis=2).reshape(Q_LORA, hw)
    wq_b = jnp.concatenate([jnp.zeros((Q_LORA, H, QK_NOPE), F32), -wq[:, :, QK_NOPE + half:],
                            wq[:, :, QK_NOPE:QK_NOPE + half], zq], axis=2).reshape(Q_LORA, hw)
    wq_p = jnp.concatenate([wq_a, wq_b], axis=1).astype(BF16)

    wkv = w_kv_up[0].reshape(KV_LORA, H, QK_NOPE + V_HEAD)
    wk = jnp.concatenate([wkv[:, :, :QK_NOPE], jnp.zeros((KV_LORA, H, HEAD_PAD - QK_NOPE), F32)],
                         axis=2).reshape(KV_LORA, hw)
    wv = wkv[:, :, QK_NOPE:].reshape(KV_LORA, H * V_HEAD)
    wkv_p = jnp.concatenate([wk, wv], axis=1).astype(BF16)

    pos = jnp.arange(S, dtype=F32)
    inv_freq = ROPE_THETA ** (-jnp.arange(0, QK_ROPE, 2, dtype=F32) / QK_ROPE)
    ang = pos[:, None] * inv_freq[None, :]
    cosv, sinv = jnp.cos(ang), jnp.sin(ang)
    cos_t = jnp.concatenate([jnp.ones((S, QK_NOPE), F32), cosv, cosv,
                             jnp.zeros((S, HEAD_PAD - QK_DIM), F32)], axis=1)
    sin_t = jnp.concatenate([jnp.zeros((S, QK_NOPE), F32), sinv, sinv,
                             jnp.zeros((S, HEAD_PAD - QK_DIM), F32)], axis=1)

    n_ts = S // TS
    hb = TS // HALO
    ypool, q, k, v = pl.pallas_call(
        functools.partial(_proj_kernel, seq_len=S),
        grid=(B, n_ts),
        in_specs=[
            pl.BlockSpec((1, TS, D), lambda b, i: (b, i, 0)),
            pl.BlockSpec((1, HALO, D), lambda b, i: (b, jnp.maximum(i * hb - 1, 0), 0)),
            pl.BlockSpec((1, HALO, D), lambda b, i: (b, jnp.minimum((i + 1) * hb, S // HALO - 1), 0)),
            _full(win_p.shape), _full((len(POOL_WINDOWS), POOL_GROUP, POOL_GROUP)),
            _full((1, POOL_WIDTH)), _full((1, Q_LORA)), _full(wq_p.shape), _full((1, KV_LORA)),
            _full(wkv_p.shape),
            pl.BlockSpec((TS, HEAD_PAD), lambda b, i: (i, 0)),
            pl.BlockSpec((TS, HEAD_PAD), lambda b, i: (i, 0)),
        ],
        out_specs=[
            pl.BlockSpec((1, TS, POOL_WIDTH), lambda b, i: (b, i, 0)),
            pl.BlockSpec((1, H, TS, HEAD_PAD), lambda b, i: (b, 0, i, 0)),
            pl.BlockSpec((1, H, TS, HEAD_PAD), lambda b, i: (b, 0, i, 0)),
            pl.BlockSpec((1, TS, H * V_HEAD), lambda b, i: (b, i, 0)),
        ],
        out_shape=[
            jax.ShapeDtypeStruct((B, S, POOL_WIDTH), BF16),
            jax.ShapeDtypeStruct((B, H, S, HEAD_PAD), BF16),
            jax.ShapeDtypeStruct((B, H, S, HEAD_PAD), BF16),
            jax.ShapeDtypeStruct((B, S, H * V_HEAD), BF16),
        ],
        scratch_shapes=[pltpu.VMEM((TS + 2 * HALO, POOL_WIDTH), F32)],
        compiler_params=_params("parallel", "arbitrary"),
        name="proj",
    )(x, x, x, win_p, w_pool[0].astype(BF16), pool_scale, q_norm_g, wq_p, kv_norm_g, wkv_p, cos_t, sin_t)

    ymla = pl.pallas_call(
        _attn_kernel,
        grid=(B, S // TQ),
        in_specs=[
            pl.BlockSpec((1, H, TQ, HEAD_PAD), lambda b, i: (b, 0, i, 0)),
            pl.BlockSpec((1, H, S, HEAD_PAD), lambda b, i: (b, 0, 0, 0)),
            pl.BlockSpec((1, S, H * V_HEAD), lambda b, i: (b, 0, 0)),
        ],
        out_specs=pl.BlockSpec((1, TQ, H * V_HEAD), lambda b, i: (b, i, 0)),
        out_shape=jax.ShapeDtypeStruct((B, S, H * V_HEAD), BF16),
        compiler_params=_params("parallel", "arbitrary"),
        name="attn",
    )(q, k, v)

    wr_t = w_router[0].T
    wr_hi = wr_t.astype(BF16)
    wr_lo = (wr_t - wr_hi.astype(F32)).astype(BF16)
    wsgu = jnp.concatenate([w_sh_gate[0], w_sh_up[0]], axis=1).astype(BF16)
    n_tm = N // TM
    row_spec = pl.BlockSpec((TM, D), lambda i: (i, 0))
    half_spec = pl.BlockSpec((TM, POOL_WIDTH), lambda i: (i, 0))
    tok_spec = pl.BlockSpec((TOP_K, TM), lambda i: (0, i))
    base, hb, lpos, top_w, tile_cnt = pl.pallas_call(
        _mix_kernel,
        grid=(n_tm,),
        in_specs=[
            row_spec, half_spec, half_spec, _full((D, D)), _full((1, D)), _full((1, D)),
            _full((N_EXPERTS, D)), _full((N_EXPERTS, D)), _full((N_EXPERTS, 1)),
            _full(wsgu.shape), _full((D_SHARED, D)),
        ],
        out_specs=[row_spec, row_spec, tok_spec, tok_spec, _full((N_EXPERTS, n_tm))],
        out_shape=[
            jax.ShapeDtypeStruct((N, D), F32),
            jax.ShapeDtypeStruct((N, D), BF16),
            jax.ShapeDtypeStruct((TOP_K, N), I32),
            jax.ShapeDtypeStruct((TOP_K, N), F32),
            jax.ShapeDtypeStruct((N_EXPERTS, n_tm), F32),
        ],
        compiler_params=_params("arbitrary"),
        name="mix",
    )(x.reshape(N, D), ypool.reshape(N, POOL_WIDTH), ymla.reshape(N, H * V_HEAD), w_out[0].astype(BF16),
      ln1_g, ln1_b, wr_hi, wr_lo, router_bias.reshape(N_EXPERTS, 1), wsgu, w_sh_down[0].astype(BF16))

    c8 = tile_cnt.T.astype(I32)
    l_start = jnp.cumsum(c8, axis=1) - c8
    expert_rows = jnp.sum(c8, axis=0)
    padded = (expert_rows + BM - 1) // BM * BM
    pad_end = jnp.cumsum(padded)
    g_start = (pad_end - padded)[None, :] + jnp.cumsum(c8, axis=0) - c8
    n_blocks = -(-(N * TOP_K + n_tm * N_EXPERTS * (SEG - 1) + N_EXPERTS * (BM - 1)) // BM)
    P = n_blocks * BM
    blk_row = jnp.arange(n_blocks, dtype=I32) * BM
    block_e = jnp.minimum(jnp.sum((pad_end[None, :] <= blk_row[:, None]).astype(I32), axis=1), N_EXPERTS - 1)
    n_used = (pad_end[-1] // BM).astype(I32).reshape(1)
    n_big = c8 // CHUNK
    big_tab, big_cnt = _chunk_table(n_big, jnp.zeros_like(c8), CHUNK, MAX_BIG, l_start, g_start)
    small_tab, small_cnt = _chunk_table((c8 - n_big * CHUNK) // SEG, n_big * CHUNK, SEG, MAX_SMALL,
                                        l_start, g_start)
    seg_tables = (big_tab, small_tab, big_cnt, small_cnt)
    fill_start = jnp.concatenate([pad_end - padded + expert_rows, pad_end[-1:]]).astype(I32)
    fill_rows = jnp.concatenate([padded - expert_rows, P - pad_end[-1:]]).astype(I32)

    buf = pl.pallas_call(
        _dispatch_kernel,
        grid_spec=pltpu.PrefetchScalarGridSpec(
            num_scalar_prefetch=6, grid=(n_tm,),
            in_specs=[pl.BlockSpec((TOP_K, TM), lambda i, *_: (0, i)),
                      pl.BlockSpec((TM, D), lambda i, *_: (i, 0))],
            out_specs=pl.BlockSpec(memory_space=pl.ANY),
            scratch_shapes=[pltpu.VMEM((2, TILE_ROWS, HALF), U32), pltpu.VMEM((CHUNK, HALF), U32),
                            pltpu.SemaphoreType.DMA((2,)), pltpu.SemaphoreType.DMA]),
        out_shape=jax.ShapeDtypeStruct((P, HALF), U32),
        compiler_params=_params("arbitrary"),
        name="dispatch",
    )(*seg_tables, fill_start, fill_rows, lpos, hb)

    def used(i, nu):
        return jnp.minimum(i, nu[0] - 1)

    obuf = pl.pallas_call(
        _ffn_kernel,
        grid_spec=pltpu.PrefetchScalarGridSpec(
            num_scalar_prefetch=2, grid=(n_blocks,),
            in_specs=[pl.BlockSpec((BM, HALF), lambda i, be, nu: (used(i, nu), 0)),
                      pl.BlockSpec((1, D, D_EXPERT), lambda i, be, nu: (be[used(i, nu)], 0, 0)),
                      pl.BlockSpec((1, D, D_EXPERT), lambda i, be, nu: (be[used(i, nu)], 0, 0)),
                      pl.BlockSpec((1, D_EXPERT, D), lambda i, be, nu: (be[used(i, nu)], 0, 0))],
            out_specs=pl.BlockSpec((BM, HALF), lambda i, be, nu: (i, 0)),
            scratch_shapes=[pltpu.VMEM((D, 2 * D_EXPERT), BF16), pltpu.VMEM((D_EXPERT, D), BF16)]),
        out_shape=jax.ShapeDtypeStruct((P, HALF), U32),
        compiler_params=_params("arbitrary"),
        name="ffn",
    )(block_e, n_used, buf, w_gate[0], w_up[0], w_down[0])

    out = pl.pallas_call(
        _combine_kernel,
        grid_spec=pltpu.PrefetchScalarGridSpec(
            num_scalar_prefetch=4, grid=(n_tm,),
            in_specs=[pl.BlockSpec((TM, TOP_K), lambda i, *_: (i, 0)),
                      pl.BlockSpec((TM, TOP_K), lambda i, *_: (i, 0)),
                      pl.BlockSpec((TM, D), lambda i, *_: (i, 0)),
                      pl.BlockSpec((1, D), lambda i, *_: (0, 0)), pl.BlockSpec((1, D), lambda i, *_: (0, 0)),
                      pl.BlockSpec(memory_space=pl.ANY)],
            out_specs=pl.BlockSpec((TM, D), lambda i, *_: (i, 0)),
            scratch_shapes=[pltpu.VMEM((2, TILE_ROWS, HALF), U32), pltpu.SemaphoreType.DMA((2,))]),
        out_shape=jax.ShapeDtypeStruct((N, D), F32),
        compiler_params=_params("arbitrary"),
        name="combine",
    )(*seg_tables, lpos.T, top_w.T, base, ln2_g, ln2_b, obuf)
    return out.reshape(B, S, D)
```

```python
import functools

import jax
import jax.numpy as jnp
from jax import lax
from jax.experimental import pallas as pl
from jax.experimental.pallas import tpu as pltpu

F32 = jnp.float32
BF16 = jnp.bfloat16
I32 = jnp.int32
U32 = jnp.uint32

D_MODEL = 1024
POOL_WINDOWS = (2, 4, 8, 16)
POOL_WIDTH = 512
POOL_GROUP = 128
MLA_HEADS = 8
QK_NOPE = 64
QK_ROPE = 32
QK_DIM = QK_NOPE + QK_ROPE
V_HEAD = 64
Q_LORA = 256
KV_LORA = 128
ROPE_THETA = 10000.0
HEAD_PAD = 128
HALO = 8

N_EXPERTS = 64
TOP_K = 8
N_GROUPS = 8
GROUP_SIZE = N_EXPERTS // N_GROUPS
TOPK_GROUPS = 4
D_EXPERT = 256
D_SHARED = 256
ROUTED_SCALE = 2.5

DEPTH = 1
ALPHA = (2 * DEPTH) ** 0.25
LN_EPS = 1e-5
RMS_EPS = 1e-6
LOG2_E = 1.4426950408889634

VMEM_LIMIT = 56 * 1024 * 1024

TS = 512
TQ = 256
TM = 256
BM = 512
HALF = D_MODEL // 2
SEG = 8
CHUNK = 32
TILE_ROWS = TM * TOP_K + N_EXPERTS * (SEG - 1)
TILE_ROWS += -TILE_ROWS % SEG
MAX_BIG = TILE_ROWS // CHUNK
MAX_SMALL = N_EXPERTS * (CHUNK // SEG - 1)
LOCAL_BITS = (TILE_ROWS // SEG - 1).bit_length()
UNROLL = 4
MAT_ROWS = 64
assert TILE_ROWS % MAT_ROWS == 0

NT_DIMS = (((1,), (1,)), ((), ()))
TN_DIMS = (((0,), (0,)), ((), ()))


def _pack_halves(lo, hi):
    ulo = lax.bitcast_convert_type(lo.astype(BF16).astype(F32), U32)
    uhi = lax.bitcast_convert_type(hi.astype(BF16).astype(F32), U32)
    return ulo | (uhi >> 16)


def _unpack_halves(p):
    lo = lax.bitcast_convert_type(p & jnp.uint32(0xFFFF0000), F32)
    hi = lax.bitcast_convert_type(p << 16, F32)
    return lo, hi


def _proj_kernel(x_ref, xl_ref, xr_ref, win_ref, wpool_ref, pscale_ref, qg_ref, wq_ref, kvg_ref,
                 wkv_ref, cos_ref, sin_ref, ypool_ref, q_ref, k_ref, v_ref, uext_ref, *, seq_len):
    i = pl.program_id(1)
    n_i = pl.num_programs(1)
    ts = x_ref.shape[1]

    proj = jnp.dot(x_ref[0].astype(BF16), win_ref[...], preferred_element_type=F32)
    u = proj[:, :POOL_WIDTH]

    w_u = win_ref[:, :POOL_WIDTH]
    ul = jnp.dot(xl_ref[0].astype(BF16), w_u, preferred_element_type=F32)
    ur = jnp.dot(xr_ref[0].astype(BF16), w_u, preferred_element_type=F32)
    uext_ref[0:HALO, :] = jnp.where(i > 0, ul, 0.0)
    uext_ref[HALO:HALO + ts, :] = u
    uext_ref[HALO + ts:HALO + ts + HALO, :] = jnp.where(i < n_i - 1, ur, 0.0)

    pos = lax.broadcasted_iota(I32, (ts, 1), 0) + i * ts
    outs = []
    for g, w in enumerate(POOL_WINDOWS):
        left = w // 2
        right = w - 1 - left
        c0, c1 = g * POOL_GROUP, (g + 1) * POOL_GROUP
        acc = uext_ref[HALO - left:HALO - left + ts, c0:c1]
        for j in range(-left + 1, right + 1):
            acc = acc + uext_ref[HALO + j:HALO + j + ts, c0:c1]
        lo = jnp.maximum(pos - left, 0)
        hi = jnp.minimum(pos + right + 1, seq_len)
        pooled = acc / (hi - lo).astype(F32) - u[:, c0:c1]
        outs.append(jnp.dot(pooled.astype(BF16), wpool_ref[g], preferred_element_type=F32))
    ypool_ref[0] = (jnp.concatenate(outs, axis=1) * pscale_ref[...]).astype(BF16)

    cosf = cos_ref[...]
    sinf = sin_ref[...]

    ql = proj[:, POOL_WIDTH:POOL_WIDTH + Q_LORA]
    qn = ql * lax.rsqrt(jnp.mean(ql * ql, axis=-1, keepdims=True) + RMS_EPS) * qg_ref[...]
    qq = jnp.dot(qn.astype(BF16), wq_ref[...], preferred_element_type=F32)
    scale = QK_DIM ** -0.5 * LOG2_E
    hw = MLA_HEADS * HEAD_PAD
    for h in range(MLA_HEADS):
        qa = qq[:, h * HEAD_PAD:(h + 1) * HEAD_PAD]
        qb = qq[:, hw + h * HEAD_PAD:hw + (h + 1) * HEAD_PAD]
        q_ref[0, h] = ((qa * cosf + qb * sinf) * scale).astype(BF16)

    c_kv = POOL_WIDTH + Q_LORA
    kvl = proj[:, c_kv:c_kv + KV_LORA]
    kvn = kvl * lax.rsqrt(jnp.mean(kvl * kvl, axis=-1, keepdims=True) + RMS_EPS) * kvg_ref[...]
    kv = jnp.dot(kvn.astype(BF16), wkv_ref[...], preferred_element_type=F32)
    c_r = c_kv + KV_LORA
    kr = proj[:, c_r:c_r + HEAD_PAD] * cosf + proj[:, c_r + HEAD_PAD:c_r + 2 * HEAD_PAD] * sinf
    for h in range(MLA_HEADS):
        k_ref[0, h] = (kv[:, h * HEAD_PAD:(h + 1) * HEAD_PAD] + kr).astype(BF16)
    v_ref[0] = kv[:, hw:].astype(BF16)


def _attn_kernel(q_ref, k_ref, v_ref, o_ref):
    tq = q_ref.shape[2]
    quad = 4 * V_HEAD
    lane = lax.broadcasted_iota(I32, (tq, quad), 1)
    for j in range(MLA_HEADS // 4):
        vq = v_ref[0, :, j * quad:(j + 1) * quad]
        acc = jnp.zeros((tq, quad), F32)
        for hh in range(4):
            h = 4 * j + hh
            s = lax.dot_general(q_ref[0, h], k_ref[0, h], NT_DIMS, preferred_element_type=F32)
            m = jnp.max(s, axis=1, keepdims=True)
            p = jnp.exp2(s - m)
            l = jnp.sum(p, axis=1, keepdims=True)
            o = jnp.dot(p.astype(BF16), vq, preferred_element_type=F32) / l
            acc = jnp.where((lane >= hh * V_HEAD) & (lane < (hh + 1) * V_HEAD), o, acc)
        o_ref[0, :, j * quad:(j + 1) * quad] = acc.astype(BF16)


def _first_argmax(vals, iota, sentinel):
    m = jnp.max(vals, axis=0, keepdims=True)
    idx = jnp.min(jnp.where(vals == m, iota, sentinel), axis=0, keepdims=True)
    return m, idx


def _mix_kernel(x_ref, yp_ref, ym_ref, wout_ref, g1_ref, b1_ref, wrh_ref, wrl_ref, rb_ref, wsgu_ref,
                wsd_ref, base_ref, hb_ref, lpos_ref, tw_ref, cnt_ref):
    i = pl.program_id(0)
    tm = x_ref.shape[0]

    @pl.when(i == 0)
    def _():
        cnt_ref[...] = jnp.zeros_like(cnt_ref)

    mixed = (jnp.dot(yp_ref[...], wout_ref[:POOL_WIDTH, :], preferred_element_type=F32)
             + jnp.dot(ym_ref[...], wout_ref[POOL_WIDTH:, :], preferred_element_type=F32))
    z = ALPHA * x_ref[...] + mixed
    mu = jnp.mean(z, axis=-1, keepdims=True)
    zc = z - mu
    var = jnp.mean(zc * zc, axis=-1, keepdims=True)
    h = zc * lax.rsqrt(var + LN_EPS) * g1_ref[...] + b1_ref[...]
    hb = h.astype(BF16)

    gu = jnp.dot(hb, wsgu_ref[...], preferred_element_type=F32)
    act = jax.nn.silu(gu[:, :D_SHARED]) * gu[:, D_SHARED:]
    shared = jnp.dot(act.astype(BF16), wsd_ref[...], preferred_element_type=F32)
    base_ref[...] = ALPHA * h + shared
    hb_ref[...] = hb

    h_lo = (h - hb.astype(F32)).astype(BF16)
    logits = (lax.dot_general(wrh_ref[...], hb, NT_DIMS, preferred_element_type=F32)
              + lax.dot_general(wrl_ref[...], hb, NT_DIMS, preferred_element_type=F32)
              + lax.dot_general(wrh_ref[...], h_lo, NT_DIMS, preferred_element_type=F32))
    scores = jax.nn.sigmoid(logits)
    biased = scores + rb_ref[...]

    neg_inf = jnp.float32(-jnp.inf)
    iota_g = lax.broadcasted_iota(I32, (GROUP_SIZE, tm), 0)
    grp_rows = []
    for g in range(N_GROUPS):
        bg = biased[g * GROUP_SIZE:(g + 1) * GROUP_SIZE, :]
        m1, i1 = _first_argmax(bg, iota_g, GROUP_SIZE)
        m2 = jnp.max(jnp.where(iota_g == i1, neg_inf, bg), axis=0, keepdims=True)
        grp_rows.append(m1 + m2)
    grp_score = jnp.concatenate(grp_rows, axis=0)

    iota_n = lax.broadcasted_iota(I32, (N_GROUPS, tm), 0)
    grp_sel = jnp.zeros((N_GROUPS, tm), jnp.bool_)
    for _ in range(TOPK_GROUPS):
        _, gi = _first_argmax(grp_score, iota_n, N_GROUPS)
        hit = iota_n == gi
        grp_sel = grp_sel | hit
        grp_score = jnp.where(hit, neg_inf, grp_score)

    cand = jnp.concatenate(
        [jnp.where(grp_sel[g:g + 1, :], biased[g * GROUP_SIZE:(g + 1) * GROUP_SIZE, :], neg_inf)
         for g in range(N_GROUPS)], axis=0)

    iota_e = lax.broadcasted_iota(I32, (N_EXPERTS, tm), 0)
    sel = jnp.zeros((N_EXPERTS, tm), jnp.bool_)
    e_rows, w_rows = [], []
    for _ in range(TOP_K):
        _, ei = _first_argmax(cand, iota_e, N_EXPERTS)
        hit = iota_e == ei
        e_rows.append(ei)
        w_rows.append(jnp.sum(jnp.where(hit, scores, 0.0), axis=0, keepdims=True))
        sel = sel | hit
        cand = jnp.where(hit, neg_inf, cand)
    top_s = jnp.concatenate(w_rows, axis=0)
    top_w = top_s / jnp.sum(top_s, axis=0, keepdims=True) * ROUTED_SCALE

    sel_f = sel.astype(F32)
    r_i = lax.broadcasted_iota(I32, (tm, tm), 0)
    c_i = lax.broadcasted_iota(I32, (tm, tm), 1)
    before = (r_i < c_i).astype(BF16)
    rank = jnp.dot(sel_f.astype(BF16), before, preferred_element_type=F32)
    count = jnp.sum(sel_f, axis=1, keepdims=True)
    granules = jnp.ceil(count * (1.0 / SEG))
    e_r = lax.broadcasted_iota(I32, (N_EXPERTS, N_EXPERTS), 0)
    e_c = lax.broadcasted_iota(I32, (N_EXPERTS, N_EXPERTS), 1)
    earlier = (e_c < e_r).astype(BF16)
    g_b = jnp.broadcast_to(granules, (N_EXPERTS, 128)).astype(BF16)
    seg_start = jnp.dot(earlier, g_b, preferred_element_type=F32)[:, 0:1] * SEG
    pos = seg_start + rank
    p_rows = [jnp.sum(jnp.where(iota_e == e_rows[kk], pos, 0.0), axis=0, keepdims=True)
              for kk in range(TOP_K)]
    lpos_ref[...] = jnp.concatenate(p_rows, axis=0).astype(I32)
    tw_ref[...] = top_w
    lane_t = lax.broadcasted_iota(I32, cnt_ref.shape, 1)
    cnt_ref[...] = jnp.where(lane_t == i, granules * SEG, cnt_ref[...])


def _position_matrix(lpos, values, out_ref):
    tm = lpos.shape[1]
    lp = lpos.astype(jnp.int16)
    vals = None if values is None else values.astype(BF16)
    iota = lax.broadcasted_iota(jnp.int16, (MAT_ROWS, tm), 0)
    for c in range(TILE_ROWS // MAT_ROWS):
        acc = jnp.zeros((MAT_ROWS, tm), BF16)
        for kk in range(TOP_K):
            hit = iota == lp[kk:kk + 1, :] - jnp.int16(c * MAT_ROWS)
            acc = jnp.where(hit, jnp.bfloat16(1.0) if vals is None else vals[kk:kk + 1, :], acc)
        out_ref[c * MAT_ROWS:(c + 1) * MAT_ROWS, :] = acc


def _for_each_tile_copy(tabs, tile, make_copy, act):
    big_ref, small_ref, nbig_ref, nsmall_ref = tabs
    for tab_ref, n_ref, cap, rows in ((big_ref, nbig_ref, MAX_BIG, CHUNK), (small_ref, nsmall_ref, MAX_SMALL, SEG)):
        def one(j, tab_ref=tab_ref, cap=cap, rows=rows):
            p = tab_ref[tile * cap + j]
            lrow = pl.multiple_of((p & (2 ** LOCAL_BITS - 1)) * SEG, SEG)
            grow = pl.multiple_of(lax.shift_right_logical(p, LOCAL_BITS) * SEG, SEG)
            act(make_copy(lrow, grow, rows))

        n = n_ref[tile]
        n_groups = lax.shift_right_logical(n, UNROLL.bit_length() - 1)

        def group(q, c, one=one):
            for u in range(UNROLL):
                one(q * UNROLL + u)
            return c

        def single(q, c, one=one, n_groups=n_groups):
            one(n_groups * UNROLL + q)
            return c

        lax.fori_loop(0, n_groups, group, 0)
        lax.fori_loop(0, n - n_groups * UNROLL, single, 0)


def _dispatch_kernel(big_ref, small_ref, nbig_ref, nsmall_ref, fs_ref, fn_ref, lpos_ref, hb_ref, buf_ref,
                     sorted_ref, mat_ref, zero_ref, sem, zsem):
    tabs = (big_ref, small_ref, nbig_ref, nsmall_ref)
    i = pl.program_id(0)
    n_i = pl.num_programs(0)
    slot = lax.rem(i, 2)
    _position_matrix(lpos_ref[...], None, mat_ref)
    s = jnp.dot(mat_ref[...], hb_ref[...], preferred_element_type=F32)
    sorted_ref[slot] = (lax.bitcast_convert_type(s[:, :HALF], U32)
                        | (lax.bitcast_convert_type(s[:, HALF:], U32) >> 16))

    def copies_from(sl):
        def make_copy(lrow, grow, rows):
            return pltpu.make_async_copy(sorted_ref.at[sl, pl.ds(lrow, rows)], buf_ref.at[pl.ds(grow, rows)],
                                         sem.at[sl])
        return make_copy

    _for_each_tile_copy(tabs, i, copies_from(slot), lambda cp: cp.start())

    @pl.when(i > 0)
    def _():
        _for_each_tile_copy(tabs, i - 1, copies_from(1 - slot), lambda cp: cp.wait())

    @pl.when(i == n_i - 1)
    def _():
        _for_each_tile_copy(tabs, i, copies_from(slot), lambda cp: cp.wait())
        zero_ref[...] = jnp.zeros_like(zero_ref)

        def zero_copy(grow, rows):
            return pltpu.make_async_copy(zero_ref.at[pl.ds(0, rows)], buf_ref.at[pl.ds(grow, rows)], zsem)

        def fill(act):
            def gap(e, carry):
                gs, n = fs_ref[e], fn_ref[e]
                n_big = lax.shift_right_logical(n, CHUNK.bit_length() - 1)
                n_small = lax.shift_right_logical(n - n_big * CHUNK, SEG.bit_length() - 1)

                def big(q, c):
                    act(zero_copy(pl.multiple_of(gs + q * CHUNK, SEG), CHUNK))
                    return c

                def small(q, c):
                    act(zero_copy(pl.multiple_of(gs + n_big * CHUNK + q * SEG, SEG), SEG))
                    return c

                lax.fori_loop(0, n_big, big, 0)
                lax.fori_loop(0, n_small, small, 0)
                return carry

            lax.fori_loop(0, fs_ref.shape[0], gap, 0)

        fill(lambda cp: cp.start())
        fill(lambda cp: cp.wait())


def _ffn_kernel(be_ref, nu_ref, x_ref, wg_ref, wu_ref, wd_ref, o_ref, wgu_s, wd_s):
    i = pl.program_id(0)
    n_used = nu_ref[0]

    @pl.when((i == 0) | ((i < n_used) & (be_ref[i] != be_ref[jnp.maximum(i - 1, 0)])))
    def _():
        wgu_s[:, :D_EXPERT] = wg_ref[0].astype(BF16)
        wgu_s[:, D_EXPERT:] = wu_ref[0].astype(BF16)
        wd_s[...] = wd_ref[0].astype(BF16)

    @pl.when(i < n_used)
    def _():
        lo, hi = _unpack_halves(x_ref[...])
        x = jnp.concatenate([lo.astype(BF16), hi.astype(BF16)], axis=1)
        gu = jnp.dot(x, wgu_s[...], preferred_element_type=F32)
        act = jax.nn.silu(gu[:, :D_EXPERT]) * gu[:, D_EXPERT:]
        y = jnp.dot(act.astype(BF16), wd_s[...], preferred_element_type=F32)
        o_ref[...] = _pack_halves(y[:, :HALF], y[:, HALF:])

    @pl.when(i >= n_used)
    def _():
        o_ref[...] = jnp.zeros_like(o_ref)


def _combine_kernel(big_ref, small_ref, nbig_ref, nsmall_ref, lpos_ref, tw_ref, base_ref, g2_ref, b2_ref,
                    obuf_ref, out_ref, sorted_ref, mat_ref, sem):
    tabs = (big_ref, small_ref, nbig_ref, nsmall_ref)
    i = pl.program_id(0)
    n_i = pl.num_programs(0)
    slot = lax.rem(i, 2)

    def copies_into(sl):
        def make_copy(lrow, grow, rows):
            return pltpu.make_async_copy(obuf_ref.at[pl.ds(grow, rows)], sorted_ref.at[sl, pl.ds(lrow, rows)],
                                         sem.at[sl])
        return make_copy

    @pl.when(i == 0)
    def _():
        sorted_ref[...] = jnp.zeros_like(sorted_ref)
        _for_each_tile_copy(tabs, i, copies_into(slot), lambda cp: cp.start())

    @pl.when(i + 1 < n_i)
    def _():
        _for_each_tile_copy(tabs, i + 1, copies_into(1 - slot), lambda cp: cp.start())

    _position_matrix(lpos_ref[...], tw_ref[...], mat_ref)

    _for_each_tile_copy(tabs, i, copies_into(slot), lambda cp: cp.wait())
    lo, hi = _unpack_halves(sorted_ref[slot])
    wb = mat_ref[...]
    routed = jnp.concatenate([lax.dot_general(wb, lo.astype(BF16), TN_DIMS, preferred_element_type=F32),
                              lax.dot_general(wb, hi.astype(BF16), TN_DIMS, preferred_element_type=F32)],
                             axis=1)
    z = base_ref[...] + routed
    mu = jnp.mean(z, axis=-1, keepdims=True)
    zc = z - mu
    var = jnp.mean(zc * zc, axis=-1, keepdims=True)
    out_ref[...] = zc * lax.rsqrt(var + LN_EPS) * g2_ref[...] + b2_ref[...]


def _full(shape):
    nd = len(shape)
    return pl.BlockSpec(shape, lambda *_: (0,) * nd)


def _chunk_table(count, first_row, rows, cap, l_start, g_start):
    end = jnp.cumsum(count, axis=1)
    q = jnp.arange(cap, dtype=I32)
    expert = jnp.minimum(jnp.sum((end[:, None, :] <= q[None, :, None]).astype(I32), axis=2), N_EXPERTS - 1)
    hit = expert[:, :, None] == jnp.arange(N_EXPERTS, dtype=I32)[None, None, :]

    def pick(a):
        return jnp.sum(jnp.where(hit, a[:, None, :], 0), axis=2)

    off = pick(first_row) + (q[None, :] - pick(end - count)) * rows
    entry = ((pick(g_start) + off) // SEG) * 2 ** LOCAL_BITS + (pick(l_start) + off) // SEG
    return entry.reshape(-1).astype(I32), end[:, -1].astype(I32)


def _params(*sem):
    return pltpu.CompilerParams(dimension_semantics=sem, vmem_limit_bytes=VMEM_LIMIT)


def kernel(x, w_in, w_pool, pool_scale, q_norm_g, w_q_up, kv_norm_g, w_kv_up, w_out, ln1_g, ln1_b,
           w_router, router_bias, w_gate, w_up, w_down, w_sh_gate, w_sh_up, w_sh_down, ln2_g, ln2_b):
    B, S, D = x.shape
    assert D == D_MODEL and S % TS == 0 and S % TQ == 0 and (B * S) % TM == 0 and TS % HALO == 0
    N = B * S
    H = MLA_HEADS
    hw = H * HEAD_PAD

    wi = w_in[0]
    c_r = POOL_WIDTH + Q_LORA + KV_LORA
    rope = wi[:, c_r:c_r + QK_ROPE]
    half = QK_ROPE // 2
    zc64 = jnp.zeros((D, QK_NOPE), F32)
    zc32 = jnp.zeros((D, HEAD_PAD - QK_DIM), F32)
    rope_a = jnp.concatenate([zc64, rope, zc32], axis=1)
    rope_b = jnp.concatenate([zc64, -rope[:, half:], rope[:, :half], zc32], axis=1)
    win_p = jnp.concatenate([wi[:, :c_r], rope_a, rope_b], axis=1).astype(BF16)

    wq = w_q_up[0].reshape(Q_LORA, H, QK_DIM)
    zq = jnp.zeros((Q_LORA, H, HEAD_PAD - QK_DIM), F32)
    wq_a = jnp.concatenate([wq, zq], axis=2).reshape(Q_LORA, hw)
    wq_b = jnp.concatenate([jnp.zeros((Q_LORA, H, QK_NOPE), F32), -wq[:, :, QK_NOPE + half:],
                            wq[:, :, QK_NOPE:QK_NOPE + half], zq], axis=2).reshape(Q_LORA, hw)
    wq_p = jnp.concatenate([wq_a, wq_b], axis=1).astype(BF16)

    wkv = w_kv_up[0].reshape(KV_LORA, H, QK_NOPE + V_HEAD)
    wk = jnp.concatenate([wkv[:, :, :QK_NOPE], jnp.zeros((KV_LORA, H, HEAD_PAD - QK_NOPE), F32)],
                         axis=2).reshape(KV_LORA, hw)
    wv = wkv[:, :, QK_NOPE:].reshape(KV_LORA, H * V_HEAD)
    wkv_p = jnp.concatenate([wk, wv], axis=1).astype(BF16)

    pos = jnp.arange(S, dtype=F32)
    inv_freq = ROPE_THETA ** (-jnp.arange(0, QK_ROPE, 2, dtype=F32) / QK_ROPE)
    ang = pos[:, None] * inv_freq[None, :]
    cosv, sinv = jnp.cos(ang), jnp.sin(ang)
    cos_t = jnp.concatenate([jnp.ones((S, QK_NOPE), F32), cosv, cosv,
                             jnp.zeros((S, HEAD_PAD - QK_DIM), F32)], axis=1)
    sin_t = jnp.concatenate([jnp.zeros((S, QK_NOPE), F32), sinv, sinv,
                             jnp.zeros((S, HEAD_PAD - QK_DIM), F32)], axis=1)

    n_ts = S // TS
    hb = TS // HALO
    ypool, q, k, v = pl.pallas_call(
        functools.partial(_proj_kernel, seq_len=S),
        grid=(B, n_ts),
        in_specs=[
            pl.BlockSpec((1, TS, D), lambda b, i: (b, i, 0)),
            pl.BlockSpec((1, HALO, D), lambda b, i: (b, jnp.maximum(i * hb - 1, 0), 0)),
            pl.BlockSpec((1, HALO, D), lambda b, i: (b, jnp.minimum((i + 1) * hb, S // HALO - 1), 0)),
            _full(win_p.shape), _full((len(POOL_WINDOWS), POOL_GROUP, POOL_GROUP)),
            _full((1, POOL_WIDTH)), _full((1, Q_LORA)), _full(wq_p.shape), _full((1, KV_LORA)),
            _full(wkv_p.shape),
            pl.BlockSpec((TS, HEAD_PAD), lambda b, i: (i, 0)),
            pl.BlockSpec((TS, HEAD_PAD), lambda b, i: (i, 0)),
        ],
        out_specs=[
            pl.BlockSpec((1, TS, POOL_WIDTH), lambda b, i: (b, i, 0)),
            pl.BlockSpec((1, H, TS, HEAD_PAD), lambda b, i: (b, 0, i, 0)),
            pl.BlockSpec((1, H, TS, HEAD_PAD), lambda b, i: (b, 0, i, 0)),
            pl.BlockSpec((1, TS, H * V_HEAD), lambda b, i: (b, i, 0)),
        ],
        out_shape=[
            jax.ShapeDtypeStruct((B, S, POOL_WIDTH), BF16),
            jax.ShapeDtypeStruct((B, H, S, HEAD_PAD), BF16),
            jax.ShapeDtypeStruct((B, H, S, HEAD_PAD), BF16),
            jax.ShapeDtypeStruct((B, S, H * V_HEAD), BF16),
        ],
        scratch_shapes=[pltpu.VMEM((TS + 2 * HALO, POOL_WIDTH), F32)],
        compiler_params=_params("parallel", "arbitrary"),
        name="proj",
    )(x, x, x, win_p, w_pool[0].astype(BF16), pool_scale, q_norm_g, wq_p, kv_norm_g, wkv_p, cos_t, sin_t)

    ymla = pl.pallas_call(
        _attn_kernel,
        grid=(B, S // TQ),
        in_specs=[
            pl.BlockSpec((1, H, TQ, HEAD_PAD), lambda b, i: (b, 0, i, 0)),
            pl.BlockSpec((1, H, S, HEAD_PAD), lambda b, i: (b, 0, 0, 0)),
            pl.BlockSpec((1, S, H * V_HEAD), lambda b, i: (b, 0, 0)),
        ],
        out_specs=pl.BlockSpec((1, TQ, H * V_HEAD), lambda b, i: (b, i, 0)),
        out_shape=jax.ShapeDtypeStruct((B, S, H * V_HEAD), BF16),
        compiler_params=_params("parallel", "arbitrary"),
        name="attn",
    )(q, k, v)

    wr_t = w_router[0].T
    wr_hi = wr_t.astype(BF16)
    wr_lo = (wr_t - wr_hi.astype(F32)).astype(BF16)
    wsgu = jnp.concatenate([w_sh_gate[0], w_sh_up[0]], axis=1).astype(BF16)
    n_tm = N // TM
    row_spec = pl.BlockSpec((TM, D), lambda i: (i, 0))
    half_spec = pl.BlockSpec((TM, POOL_WIDTH), lambda i: (i, 0))
    tok_spec = pl.BlockSpec((TOP_K, TM), lambda i: (0, i))
    base, hb, lpos, top_w, tile_cnt = pl.pallas_call(
        _mix_kernel,
        grid=(n_tm,),
        in_specs=[
            row_spec, half_spec, half_spec, _full((D, D)), _full((1, D)), _full((1, D)),
            _full((N_EXPERTS, D)), _full((N_EXPERTS, D)), _full((N_EXPERTS, 1)),
            _full(wsgu.shape), _full((D_SHARED, D)),
        ],
        out_specs=[row_spec, row_spec, tok_spec, tok_spec, _full((N_EXPERTS, n_tm))],
        out_shape=[
            jax.ShapeDtypeStruct((N, D), F32),
            jax.ShapeDtypeStruct((N, D), BF16),
            jax.ShapeDtypeStruct((TOP_K, N), I32),
            jax.ShapeDtypeStruct((TOP_K, N), F32),
            jax.ShapeDtypeStruct((N_EXPERTS, n_tm), F32),
        ],
        compiler_params=_params("arbitrary"),
        name="mix",
    )(x.reshape(N, D), ypool.reshape(N, POOL_WIDTH), ymla.reshape(N, H * V_HEAD), w_out[0].astype(BF16),
      ln1_g, ln1_b, wr_hi, wr_lo, router_bias.reshape(N_EXPERTS, 1), wsgu, w_sh_down[0].astype(BF16))

    c8 = tile_cnt.T.astype(I32)
    l_start = jnp.cumsum(c8, axis=1) - c8
    expert_rows = jnp.sum(c8, axis=0)
    padded = (expert_rows + BM - 1) // BM * BM
    pad_end = jnp.cumsum(padded)
    g_start = (pad_end - padded)[None, :] + jnp.cumsum(c8, axis=0) - c8
    n_blocks = -(-(N * TOP_K + n_tm * N_EXPERTS * (SEG - 1) + N_EXPERTS * (BM - 1)) // BM)
    P = n_blocks * BM
    blk_row = jnp.arange(n_blocks, dtype=I32) * BM
    block_e = jnp.minimum(jnp.sum((pad_end[None, :] <= blk_row[:, None]).astype(I32), axis=1), N_EXPERTS - 1)
    n_used = (pad_end[-1] // BM).astype(I32).reshape(1)
    n_big = c8 // CHUNK
    big_tab, big_cnt = _chunk_table(n_big, jnp.zeros_like(c8), CHUNK, MAX_BIG, l_start, g_start)
    small_tab, small_cnt = _chunk_table((c8 - n_big * CHUNK) // SEG, n_big * CHUNK, SEG, MAX_SMALL,
                                        l_start, g_start)
    seg_tables = (big_tab, small_tab, big_cnt, small_cnt)
    fill_start = jnp.concatenate([pad_end - padded + expert_rows, pad_end[-1:]]).astype(I32)
    fill_rows = jnp.concatenate([padded - expert_rows, P - pad_end[-1:]]).astype(I32)

    buf = pl.pallas_call(
        _dispatch_kernel,
        grid_spec=pltpu.PrefetchScalarGridSpec(
            num_scalar_prefetch=6, grid=(n_tm,),
            in_specs=[pl.BlockSpec((TOP_K, TM), lambda i, *_: (0, i)),
                      pl.BlockSpec((TM, D), lambda i, *_: (i, 0))],
            out_specs=pl.BlockSpec(memory_space=pl.ANY),
            scratch_shapes=[pltpu.VMEM((2, TILE_ROWS, HALF), U32), pltpu.VMEM((TILE_ROWS, TM), BF16),
                            pltpu.VMEM((CHUNK, HALF), U32),
                            pltpu.SemaphoreType.DMA((2,)), pltpu.SemaphoreType.DMA]),
        out_shape=jax.ShapeDtypeStruct((P, HALF), U32),
        compiler_params=_params("arbitrary"),
        name="dispatch",
    )(*seg_tables, fill_start, fill_rows, lpos, hb)

    def used(i, nu):
        return jnp.minimum(i, nu[0] - 1)

    obuf = pl.pallas_call(
        _ffn_kernel,
        grid_spec=pltpu.PrefetchScalarGridSpec(
            num_scalar_prefetch=2, grid=(n_blocks,),
            in_specs=[pl.BlockSpec((BM, HALF), lambda i, be, nu: (used(i, nu), 0)),
                      pl.BlockSpec((1, D, D_EXPERT), lambda i, be, nu: (be[used(i, nu)], 0, 0)),
                      pl.BlockSpec((1, D, D_EXPERT), lambda i, be, nu: (be[used(i, nu)], 0, 0)),
                      pl.BlockSpec((1, D_EXPERT, D), lambda i, be, nu: (be[used(i, nu)], 0, 0))],
            out_specs=pl.BlockSpec((BM, HALF), lambda i, be, nu: (i, 0)),
            scratch_shapes=[pltpu.VMEM((D, 2 * D_EXPERT), BF16), pltpu.VMEM((D_EXPERT, D), BF16)]),
        out_shape=jax.ShapeDtypeStruct((P, HALF), U32),
        compiler_params=_params("arbitrary"),
        name="ffn",
    )(block_e, n_used, buf, w_gate[0], w_up[0], w_down[0])

    out = pl.pallas_call(
        _combine_kernel,
        grid_spec=pltpu.PrefetchScalarGridSpec(
            num_scalar_prefetch=4, grid=(n_tm,),
            in_specs=[pl.BlockSpec((TOP_K, TM), lambda i, *_: (0, i)),
                      pl.BlockSpec((TOP_K, TM), lambda i, *_: (0, i)),
                      pl.BlockSpec((TM, D), lambda i, *_: (i, 0)),
                      pl.BlockSpec((1, D), lambda i, *_: (0, 0)), pl.BlockSpec((1, D), lambda i, *_: (0, 0)),
                      pl.BlockSpec(memory_space=pl.ANY)],
            out_specs=pl.BlockSpec((TM, D), lambda i, *_: (i, 0)),
            scratch_shapes=[pltpu.VMEM((2, TILE_ROWS, HALF), U32), pltpu.VMEM((TILE_ROWS, TM), BF16),
                            pltpu.SemaphoreType.DMA((2,))]),
        out_shape=jax.ShapeDtypeStruct((N, D), F32),
        compiler_params=_params("arbitrary"),
        name="combine",
    )(*seg_tables, lpos, top_w, base, ln2_g, ln2_b, obuf)
    return out.reshape(B, S, D)
```

```python
import functools

import jax
import jax.numpy as jnp
from jax import lax
from jax.experimental import pallas as pl
from jax.experimental.pallas import tpu as pltpu

F32 = jnp.float32
BF16 = jnp.bfloat16
I32 = jnp.int32
U32 = jnp.uint32

D_MODEL = 1024
POOL_WINDOWS = (2, 4, 8, 16)
POOL_WIDTH = 512
POOL_GROUP = 128
MLA_HEADS = 8
QK_NOPE = 64
QK_ROPE = 32
QK_DIM = QK_NOPE + QK_ROPE
V_HEAD = 64
Q_LORA = 256
KV_LORA = 128
ROPE_THETA = 10000.0
HEAD_PAD = 128
HALO = 8

N_EXPERTS = 64
TOP_K = 8
N_GROUPS = 8
GROUP_SIZE = N_EXPERTS // N_GROUPS
TOPK_GROUPS = 4
D_EXPERT = 256
D_SHARED = 256
ROUTED_SCALE = 2.5

DEPTH = 1
ALPHA = (2 * DEPTH) ** 0.25
LN_EPS = 1e-5
RMS_EPS = 1e-6
LOG2_E = 1.4426950408889634

VMEM_LIMIT = 56 * 1024 * 1024

TS = 512
TQ = 512
TM = 256
BM = 512
HALF = D_MODEL // 2
SEG = 8
CHUNK = 32
TILE_ROWS = TM * TOP_K + N_EXPERTS * (SEG - 1)
TILE_ROWS += -TILE_ROWS % SEG
MAX_BIG = TILE_ROWS // CHUNK
MAX_SMALL = N_EXPERTS * (CHUNK // SEG - 1)
LOCAL_BITS = (TILE_ROWS // SEG - 1).bit_length()
UNROLL = 4
MAT_ROWS = 64
assert TILE_ROWS % MAT_ROWS == 0

NT_DIMS = (((1,), (1,)), ((), ()))
TN_DIMS = (((0,), (0,)), ((), ()))


def _pack_halves(lo, hi):
    ulo = lax.bitcast_convert_type(lo.astype(BF16).astype(F32), U32)
    uhi = lax.bitcast_convert_type(hi.astype(BF16).astype(F32), U32)
    return ulo | (uhi >> 16)


def _unpack_halves(p):
    lo = lax.bitcast_convert_type(p & jnp.uint32(0xFFFF0000), F32)
    hi = lax.bitcast_convert_type(p << 16, F32)
    return lo, hi


def _proj_kernel(x_ref, xl_ref, xr_ref, win_ref, wpool_ref, pscale_ref, qg_ref, wq_ref, kvg_ref,
                 wk_ref, wvt_ref, cos_ref, sin_ref, ypool_ref, q_ref, k_ref, vt_ref, uext_ref, *, seq_len):
    i = pl.program_id(1)
    n_i = pl.num_programs(1)
    ts = x_ref.shape[1]

    proj = jnp.dot(x_ref[0].astype(BF16), win_ref[...], preferred_element_type=F32)
    u = proj[:, :POOL_WIDTH]

    w_u = win_ref[:, :POOL_WIDTH]
    ul = jnp.dot(xl_ref[0].astype(BF16), w_u, preferred_element_type=F32)
    ur = jnp.dot(xr_ref[0].astype(BF16), w_u, preferred_element_type=F32)
    uext_ref[0:HALO, :] = jnp.where(i > 0, ul, 0.0)
    uext_ref[HALO:HALO + ts, :] = u
    uext_ref[HALO + ts:HALO + ts + HALO, :] = jnp.where(i < n_i - 1, ur, 0.0)

    pos = lax.broadcasted_iota(I32, (ts, 1), 0) + i * ts
    outs = []
    for g, w in enumerate(POOL_WINDOWS):
        left = w // 2
        right = w - 1 - left
        c0, c1 = g * POOL_GROUP, (g + 1) * POOL_GROUP
        acc = uext_ref[HALO - left:HALO - left + ts, c0:c1]
        for j in range(-left + 1, right + 1):
            acc = acc + uext_ref[HALO + j:HALO + j + ts, c0:c1]
        lo = jnp.maximum(pos - left, 0)
        hi = jnp.minimum(pos + right + 1, seq_len)
        pooled = acc / (hi - lo).astype(F32) - u[:, c0:c1]
        outs.append(jnp.dot(pooled.astype(BF16), wpool_ref[g], preferred_element_type=F32))
    ypool_ref[0] = (jnp.concatenate(outs, axis=1) * pscale_ref[...]).astype(BF16)

    cosf = cos_ref[...]
    sinf = sin_ref[...]

    ql = proj[:, POOL_WIDTH:POOL_WIDTH + Q_LORA]
    qn = ql * lax.rsqrt(jnp.mean(ql * ql, axis=-1, keepdims=True) + RMS_EPS) * qg_ref[...]
    qq = jnp.dot(qn.astype(BF16), wq_ref[...], preferred_element_type=F32)
    scale = QK_DIM ** -0.5 * LOG2_E
    hw = MLA_HEADS * HEAD_PAD
    for h in range(MLA_HEADS):
        qa = qq[:, h * HEAD_PAD:(h + 1) * HEAD_PAD]
        qb = qq[:, hw + h * HEAD_PAD:hw + (h + 1) * HEAD_PAD]
        q_ref[0, h] = ((qa * cosf + qb * sinf) * scale).astype(BF16)

    c_kv = POOL_WIDTH + Q_LORA
    kvl = proj[:, c_kv:c_kv + KV_LORA]
    kvn = kvl * lax.rsqrt(jnp.mean(kvl * kvl, axis=-1, keepdims=True) + RMS_EPS) * kvg_ref[...]
    kvb = kvn.astype(BF16)
    kn = jnp.dot(kvb, wk_ref[...], preferred_element_type=F32)
    c_r = c_kv + KV_LORA
    kr = proj[:, c_r:c_r + HEAD_PAD] * cosf + proj[:, c_r + HEAD_PAD:c_r + 2 * HEAD_PAD] * sinf
    for h in range(MLA_HEADS):
        k_ref[0, h] = (kn[:, h * HEAD_PAD:(h + 1) * HEAD_PAD] + kr).astype(BF16)
    vt_ref[0] = lax.dot_general(wvt_ref[...], kvb, NT_DIMS, preferred_element_type=F32).astype(BF16)


def _attn_kernel(q_ref, k_ref, vt_ref, o_ref):
    s_len = k_ref.shape[2]
    ones = jnp.ones((16, s_len), BF16)

    def scores(h):
        return lax.dot_general(k_ref[0, h], q_ref[0, h], NT_DIMS, preferred_element_type=F32)

    st_next = scores(0)
    outs = []
    for h in range(MLA_HEADS):
        st = st_next
        if h + 1 < MLA_HEADS:
            st_next = scores(h + 1)
        p = jnp.exp2(st - jnp.max(st, axis=0, keepdims=True)).astype(BF16)
        v_aug = jnp.concatenate([vt_ref[0, h * V_HEAD:(h + 1) * V_HEAD, :], ones], axis=0)
        ot = jnp.dot(v_aug, p, preferred_element_type=F32)
        outs.append(ot[:V_HEAD] / ot[V_HEAD:V_HEAD + 1])
    o_ref[0] = jnp.concatenate(outs, axis=0).T.astype(BF16)


def _first_argmax(vals, iota, sentinel):
    m = jnp.max(vals, axis=0, keepdims=True)
    idx = jnp.min(jnp.where(vals == m, iota, sentinel), axis=0, keepdims=True)
    return m, idx


def _mix_kernel(x_ref, yp_ref, ym_ref, wout_ref, g1_ref, b1_ref, wrh_ref, wrl_ref, rb_ref, wsgu_ref,
                wsd_ref, base_ref, hb_ref, lpos_ref, tw_ref, cnt_ref):
    i = pl.program_id(0)
    tm = x_ref.shape[0]

    @pl.when(i == 0)
    def _():
        cnt_ref[...] = jnp.zeros_like(cnt_ref)

    mixed = (jnp.dot(yp_ref[...], wout_ref[:POOL_WIDTH, :], preferred_element_type=F32)
             + jnp.dot(ym_ref[...], wout_ref[POOL_WIDTH:, :], preferred_element_type=F32))
    z = ALPHA * x_ref[...] + mixed
    mu = jnp.mean(z, axis=-1, keepdims=True)
    zc = z - mu
    var = jnp.mean(zc * zc, axis=-1, keepdims=True)
    h = zc * lax.rsqrt(var + LN_EPS) * g1_ref[...] + b1_ref[...]
    hb = h.astype(BF16)

    hb_ref[...] = hb

    h_lo = (h - hb.astype(F32)).astype(BF16)
    logits = (lax.dot_general(wrh_ref[...], hb, NT_DIMS, preferred_element_type=F32)
              + lax.dot_general(wrl_ref[...], hb, NT_DIMS, preferred_element_type=F32)
              + lax.dot_general(wrh_ref[...], h_lo, NT_DIMS, preferred_element_type=F32))
    scores = jax.nn.sigmoid(logits)
    biased = scores + rb_ref[...]

    neg_inf = jnp.float32(-jnp.inf)
    iota_g = lax.broadcasted_iota(I32, (GROUP_SIZE, tm), 0)
    grp_rows = []
    for g in range(N_GROUPS):
        bg = biased[g * GROUP_SIZE:(g + 1) * GROUP_SIZE, :]
        m1, i1 = _first_argmax(bg, iota_g, GROUP_SIZE)
        m2 = jnp.max(jnp.where(iota_g == i1, neg_inf, bg), axis=0, keepdims=True)
        grp_rows.append(m1 + m2)
    grp_score = jnp.concatenate(grp_rows, axis=0)

    iota_n = lax.broadcasted_iota(I32, (N_GROUPS, tm), 0)
    grp_sel = jnp.zeros((N_GROUPS, tm), jnp.bool_)
    for _ in range(TOPK_GROUPS):
        _, gi = _first_argmax(grp_score, iota_n, N_GROUPS)
        hit = iota_n == gi
        grp_sel = grp_sel | hit
        grp_score = jnp.where(hit, neg_inf, grp_score)

    cand = jnp.concatenate(
        [jnp.where(grp_sel[g:g + 1, :], biased[g * GROUP_SIZE:(g + 1) * GROUP_SIZE, :], neg_inf)
         for g in range(N_GROUPS)], axis=0)

    iota_e = lax.broadcasted_iota(I32, (N_EXPERTS, tm), 0)
    sel = jnp.zeros((N_EXPERTS, tm), jnp.bool_)
    e_rows, w_rows = [], []
    for _ in range(TOP_K):
        _, ei = _first_argmax(cand, iota_e, N_EXPERTS)
        hit = iota_e == ei
        e_rows.append(ei)
        w_rows.append(jnp.sum(jnp.where(hit, scores, 0.0), axis=0, keepdims=True))
        sel = sel | hit
        cand = jnp.where(hit, neg_inf, cand)
    top_s = jnp.concatenate(w_rows, axis=0)
    top_w = top_s / jnp.sum(top_s, axis=0, keepdims=True) * ROUTED_SCALE

    sel_f = sel.astype(F32)
    r_i = lax.broadcasted_iota(I32, (tm, tm), 0)
    c_i = lax.broadcasted_iota(I32, (tm, tm), 1)
    before = (r_i < c_i).astype(BF16)
    rank = jnp.dot(sel_f.astype(BF16), before, preferred_element_type=F32)
    count = jnp.sum(sel_f, axis=1, keepdims=True)
    granules = jnp.ceil(count * (1.0 / SEG))
    e_r = lax.broadcasted_iota(I32, (N_EXPERTS, N_EXPERTS), 0)
    e_c = lax.broadcasted_iota(I32, (N_EXPERTS, N_EXPERTS), 1)
    earlier = (e_c < e_r).astype(BF16)
    g_b = jnp.broadcast_to(granules, (N_EXPERTS, 128)).astype(BF16)
    seg_start = jnp.dot(earlier, g_b, preferred_element_type=F32)[:, 0:1] * SEG
    pos = seg_start + rank
    p_rows = [jnp.sum(jnp.where(iota_e == e_rows[kk], pos, 0.0), axis=0, keepdims=True)
              for kk in range(TOP_K)]
    lpos_ref[...] = jnp.concatenate(p_rows, axis=0).astype(I32)
    tw_ref[...] = top_w
    lane_t = lax.broadcasted_iota(I32, cnt_ref.shape, 1)
    cnt_ref[...] = jnp.where(lane_t == i, granules * SEG, cnt_ref[...])

    gu = jnp.dot(hb, wsgu_ref[...], preferred_element_type=F32)
    act = jax.nn.silu(gu[:, :D_SHARED]) * gu[:, D_SHARED:]
    shared = jnp.dot(act.astype(BF16), wsd_ref[...], preferred_element_type=F32)
    base_ref[...] = ALPHA * h + shared


def _position_matrix(lpos, values, out_ref):
    tm = lpos.shape[1]
    lp = lpos.astype(jnp.int16)
    vals = None if values is None else values.astype(BF16)
    iota = lax.broadcasted_iota(jnp.int16, (MAT_ROWS, tm), 0)
    for c in range(TILE_ROWS // MAT_ROWS):
        acc = jnp.zeros((MAT_ROWS, tm), BF16)
        for kk in range(TOP_K):
            hit = iota == lp[kk:kk + 1, :] - jnp.int16(c * MAT_ROWS)
            acc = jnp.where(hit, jnp.bfloat16(1.0) if vals is None else vals[kk:kk + 1, :], acc)
        out_ref[c * MAT_ROWS:(c + 1) * MAT_ROWS, :] = acc


def _for_each_tile_copy(tabs, tile, make_copy, act):
    big_ref, small_ref, nbig_ref, nsmall_ref = tabs
    for tab_ref, n_ref, cap, rows in ((big_ref, nbig_ref, MAX_BIG, CHUNK), (small_ref, nsmall_ref, MAX_SMALL, SEG)):
        def one(j, tab_ref=tab_ref, cap=cap, rows=rows):
            p = tab_ref[tile * cap + j]
            lrow = pl.multiple_of((p & (2 ** LOCAL_BITS - 1)) * SEG, SEG)
            grow = pl.multiple_of(lax.shift_right_logical(p, LOCAL_BITS) * SEG, SEG)
            act(make_copy(lrow, grow, rows))

        n = n_ref[tile]
        n_groups = lax.shift_right_logical(n, UNROLL.bit_length() - 1)

        def group(q, c, one=one):
            for u in range(UNROLL):
                one(q * UNROLL + u)
            return c

        def single(q, c, one=one, n_groups=n_groups):
            one(n_groups * UNROLL + q)
            return c

        lax.fori_loop(0, n_groups, group, 0)
        lax.fori_loop(0, n - n_groups * UNROLL, single, 0)


def _dispatch_kernel(big_ref, small_ref, nbig_ref, nsmall_ref, fs_ref, fn_ref, lpos_ref, hb_ref, buf_ref,
                     sorted_ref, mat_ref, zero_ref, sem, zsem):
    tabs = (big_ref, small_ref, nbig_ref, nsmall_ref)
    i = pl.program_id(0)
    n_i = pl.num_programs(0)
    slot = lax.rem(i, 2)
    _position_matrix(lpos_ref[...], None, mat_ref)
    s = jnp.dot(mat_ref[...], hb_ref[...], preferred_element_type=F32)
    sorted_ref[slot] = (lax.bitcast_convert_type(s[:, :HALF], U32)
                        | (lax.bitcast_convert_type(s[:, HALF:], U32) >> 16))

    def copies_from(sl):
        def make_copy(lrow, grow, rows):
            return pltpu.make_async_copy(sorted_ref.at[sl, pl.ds(lrow, rows)], buf_ref.at[pl.ds(grow, rows)],
                                         sem.at[sl])
        return make_copy

    _for_each_tile_copy(tabs, i, copies_from(slot), lambda cp: cp.start())

    @pl.when(i > 0)
    def _():
        _for_each_tile_copy(tabs, i - 1, copies_from(1 - slot), lambda cp: cp.wait())

    @pl.when(i == n_i - 1)
    def _():
        _for_each_tile_copy(tabs, i, copies_from(slot), lambda cp: cp.wait())
        zero_ref[...] = jnp.zeros_like(zero_ref)

        def zero_copy(grow, rows):
            return pltpu.make_async_copy(zero_ref.at[pl.ds(0, rows)], buf_ref.at[pl.ds(grow, rows)], zsem)

        def fill(act):
            def gap(e, carry):
                gs, n = fs_ref[e], fn_ref[e]
                n_big = lax.shift_right_logical(n, CHUNK.bit_length() - 1)
                n_small = lax.shift_right_logical(n - n_big * CHUNK, SEG.bit_length() - 1)

                def big(q, c):
                    act(zero_copy(pl.multiple_of(gs + q * CHUNK, SEG), CHUNK))
                    return c

                def small(q, c):
                    act(zero_copy(pl.multiple_of(gs + n_big * CHUNK + q * SEG, SEG), SEG))
                    return c

                lax.fori_loop(0, n_big, big, 0)
                lax.fori_loop(0, n_small, small, 0)
                return carry

            lax.fori_loop(0, fs_ref.shape[0], gap, 0)

        fill(lambda cp: cp.start())
        fill(lambda cp: cp.wait())


def _ffn_kernel(be_ref, nu_ref, x_ref, wg_ref, wu_ref, wd_ref, o_ref, wgu_s, wd_s):
    i = pl.program_id(0)
    n_used = nu_ref[0]

    @pl.when((i == 0) | ((i < n_used) & (be_ref[i] != be_ref[jnp.maximum(i - 1, 0)])))
    def _():
        wgu_s[:, :D_EXPERT] = wg_ref[0].astype(BF16)
        wgu_s[:, D_EXPERT:] = wu_ref[0].astype(BF16)
        wd_s[...] = wd_ref[0].astype(BF16)

    @pl.when(i < n_used)
    def _():
        lo, hi = _unpack_halves(x_ref[...])
        x = jnp.concatenate([lo.astype(BF16), hi.astype(BF16)], axis=1)
        gu = jnp.dot(x, wgu_s[...], preferred_element_type=F32)
        act = jax.nn.silu(gu[:, :D_EXPERT]) * gu[:, D_EXPERT:]
        y = jnp.dot(act.astype(BF16), wd_s[...], preferred_element_type=F32)
        o_ref[...] = _pack_halves(y[:, :HALF], y[:, HALF:])

    @pl.when(i >= n_used)
    def _():
        o_ref[...] = jnp.zeros_like(o_ref)


def _combine_kernel(big_ref, small_ref, nbig_ref, nsmall_ref, lpos_ref, tw_ref, base_ref, g2_ref, b2_ref,
                    obuf_ref, out_ref, sorted_ref, mat_ref, sem):
    tabs = (big_ref, small_ref, nbig_ref, nsmall_ref)
    i = pl.program_id(0)
    n_i = pl.num_programs(0)
    slot = lax.rem(i, 2)

    def copies_into(sl):
        def make_copy(lrow, grow, rows):
            return pltpu.make_async_copy(obuf_ref.at[pl.ds(grow, rows)], sorted_ref.at[sl, pl.ds(lrow, rows)],
                                         sem.at[sl])
        return make_copy

    @pl.when(i == 0)
    def _():
        sorted_ref[...] = jnp.zeros_like(sorted_ref)
        _for_each_tile_copy(tabs, i, copies_into(slot), lambda cp: cp.start())

    @pl.when(i + 1 < n_i)
    def _():
        _for_each_tile_copy(tabs, i + 1, copies_into(1 - slot), lambda cp: cp.start())

    _position_matrix(lpos_ref[...], tw_ref[...], mat_ref)

    _for_each_tile_copy(tabs, i, copies_into(slot), lambda cp: cp.wait())
    lo, hi = _unpack_halves(sorted_ref[slot])
    wb = mat_ref[...]
    routed = jnp.concatenate([lax.dot_general(wb, lo.astype(BF16), TN_DIMS, preferred_element_type=F32),
                              lax.dot_general(wb, hi.astype(BF16), TN_DIMS, preferred_element_type=F32)],
                             axis=1)
    z = base_ref[...] + routed
    mu = jnp.mean(z, axis=-1, keepdims=True)
    zc = z - mu
    var = jnp.mean(zc * zc, axis=-1, keepdims=True)
    out_ref[...] = zc * lax.rsqrt(var + LN_EPS) * g2_ref[...] + b2_ref[...]


def _full(shape):
    nd = len(shape)
    return pl.BlockSpec(shape, lambda *_: (0,) * nd)


def _chunk_table(count, first_row, rows, cap, l_start, g_start):
    end = jnp.cumsum(count, axis=1)
    q = jnp.arange(cap, dtype=I32)
    expert = jnp.minimum(jnp.sum((end[:, None, :] <= q[None, :, None]).astype(I32), axis=2), N_EXPERTS - 1)
    hit = expert[:, :, None] == jnp.arange(N_EXPERTS, dtype=I32)[None, None, :]

    def pick(a):
        return jnp.sum(jnp.where(hit, a[:, None, :], 0), axis=2)

    off = pick(first_row) + (q[None, :] - pick(end - count)) * rows
    entry = ((pick(g_start) + off) // SEG) * 2 ** LOCAL_BITS + (pick(l_start) + off) // SEG
    return entry.reshape(-1).astype(I32), end[:, -1].astype(I32)


def _params(*sem):
    return pltpu.CompilerParams(dimension_semantics=sem, vmem_limit_bytes=VMEM_LIMIT)


def kernel(x, w_in, w_pool, pool_scale, q_norm_g, w_q_up, kv_norm_g, w_kv_up, w_out, ln1_g, ln1_b,
           w_router, router_bias, w_gate, w_up, w_down, w_sh_gate, w_sh_up, w_sh_down, ln2_g, ln2_b):
    B, S, D = x.shape
    assert D == D_MODEL and S % TS == 0 and S % TQ == 0 and (B * S) % TM == 0 and TS % HALO == 0
    N = B * S
    H = MLA_HEADS
    hw = H * HEAD_PAD

    wi = w_in[0]
    c_r = POOL_WIDTH + Q_LORA + KV_LORA
    rope = wi[:, c_r:c_r + QK_ROPE]
    half = QK_ROPE // 2
    zc64 = jnp.zeros((D, QK_NOPE), F32)
    zc32 = jnp.zeros((D, HEAD_PAD - QK_DIM), F32)
    rope_a = jnp.concatenate([zc64, rope, zc32], axis=1)
    rope_b = jnp.concatenate([zc64, -rope[:, half:], rope[:, :half], zc32], axis=1)
    win_p = jnp.concatenate([wi[:, :c_r], rope_a, rope_b], axis=1).astype(BF16)

    wq = w_q_up[0].reshape(Q_LORA, H, QK_DIM)
    zq = jnp.zeros((Q_LORA, H, HEAD_PAD - QK_DIM), F32)
    wq_a = jnp.concatenate([wq, zq], axis=2).reshape(Q_LORA, hw)
    wq_b = jnp.concatenate([jnp.zeros((Q_LORA, H, QK_NOPE), F32), -wq[:, :, QK_NOPE + half:],
                            wq[:, :, QK_NOPE:QK_NOPE + half], zq], axis=2).reshape(Q_LORA, hw)
    wq_p = jnp.concatenate([wq_a, wq_b], axis=1).astype(BF16)

    wkv = w_kv_up[0].reshape(KV_LORA, H, QK_NOPE + V_HEAD)
    wk = jnp.concatenate([wkv[:, :, :QK_NOPE], jnp.zeros((KV_LORA, H, HEAD_PAD - QK_NOPE), F32)],
                         axis=2).reshape(KV_LORA, hw)
    wk_p = wk.astype(BF16)
    wvt_p = wkv[:, :, QK_NOPE:].reshape(KV_LORA, H * V_HEAD).T.astype(BF16)

    pos = jnp.arange(S, dtype=F32)
    inv_freq = ROPE_THETA ** (-jnp.arange(0, QK_ROPE, 2, dtype=F32) / QK_ROPE)
    ang = pos[:, None] * inv_freq[None, :]
    cosv, sinv = jnp.cos(ang), jnp.sin(ang)
    cos_t = jnp.concatenate([jnp.ones((S, QK_NOPE), F32), cosv, cosv,
                             jnp.zeros((S, HEAD_PAD - QK_DIM), F32)], axis=1)
    sin_t = jnp.concatenate([jnp.zeros((S, QK_NOPE), F32), sinv, sinv,
                             jnp.zeros((S, HEAD_PAD - QK_DIM), F32)], axis=1)

    n_ts = S // TS
    hb = TS // HALO
    ypool, q, k, vt = pl.pallas_call(
        functools.partial(_proj_kernel, seq_len=S),
        grid=(B, n_ts),
        in_specs=[
            pl.BlockSpec((1, TS, D), lambda b, i: (b, i, 0)),
            pl.BlockSpec((1, HALO, D), lambda b, i: (b, jnp.maximum(i * hb - 1, 0), 0)),
            pl.BlockSpec((1, HALO, D), lambda b, i: (b, jnp.minimum((i + 1) * hb, S // HALO - 1), 0)),
            _full(win_p.shape), _full((len(POOL_WINDOWS), POOL_GROUP, POOL_GROUP)),
            _full((1, POOL_WIDTH)), _full((1, Q_LORA)), _full(wq_p.shape), _full((1, KV_LORA)),
            _full(wk_p.shape), _full(wvt_p.shape),
            pl.BlockSpec((TS, HEAD_PAD), lambda b, i: (i, 0)),
            pl.BlockSpec((TS, HEAD_PAD), lambda b, i: (i, 0)),
        ],
        out_specs=[
            pl.BlockSpec((1, TS, POOL_WIDTH), lambda b, i: (b, i, 0)),
            pl.BlockSpec((1, H, TS, HEAD_PAD), lambda b, i: (b, 0, i, 0)),
            pl.BlockSpec((1, H, TS, HEAD_PAD), lambda b, i: (b, 0, i, 0)),
            pl.BlockSpec((1, H * V_HEAD, TS), lambda b, i: (b, 0, i)),
        ],
        out_shape=[
            jax.ShapeDtypeStruct((B, S, POOL_WIDTH), BF16),
            jax.ShapeDtypeStruct((B, H, S, HEAD_PAD), BF16),
            jax.ShapeDtypeStruct((B, H, S, HEAD_PAD), BF16),
            jax.ShapeDtypeStruct((B, H * V_HEAD, S), BF16),
        ],
        scratch_shapes=[pltpu.VMEM((TS + 2 * HALO, POOL_WIDTH), F32)],
        compiler_params=_params("parallel", "arbitrary"),
        name="proj",
    )(x, x, x, win_p, w_pool[0].astype(BF16), pool_scale, q_norm_g, wq_p, kv_norm_g, wk_p, wvt_p,
      cos_t, sin_t)

    ymla = pl.pallas_call(
        _attn_kernel,
        grid=(B, S // TQ),
        in_specs=[
            pl.BlockSpec((1, H, TQ, HEAD_PAD), lambda b, i: (b, 0, i, 0)),
            pl.BlockSpec((1, H, S, HEAD_PAD), lambda b, i: (b, 0, 0, 0), pipeline_mode=pl.Buffered(1)),
            pl.BlockSpec((1, H * V_HEAD, S), lambda b, i: (b, 0, 0), pipeline_mode=pl.Buffered(1)),
        ],
        out_specs=pl.BlockSpec((1, TQ, H * V_HEAD), lambda b, i: (b, i, 0)),
        out_shape=jax.ShapeDtypeStruct((B, S, H * V_HEAD), BF16),
        compiler_params=_params("parallel", "arbitrary"),
        name="attn",
    )(q, k, vt)

    wr_t = w_router[0].T
    wr_hi = wr_t.astype(BF16)
    wr_lo = (wr_t - wr_hi.astype(F32)).astype(BF16)
    wsgu = jnp.concatenate([w_sh_gate[0], w_sh_up[0]], axis=1).astype(BF16)
    n_tm = N // TM
    row_spec = pl.BlockSpec((TM, D), lambda i: (i, 0))
    half_spec = pl.BlockSpec((TM, POOL_WIDTH), lambda i: (i, 0))
    tok_spec = pl.BlockSpec((TOP_K, TM), lambda i: (0, i))
    base, hb, lpos, top_w, tile_cnt = pl.pallas_call(
        _mix_kernel,
        grid=(n_tm,),
        in_specs=[
            row_spec, half_spec, half_spec, _full((D, D)), _full((1, D)), _full((1, D)),
            _full((N_EXPERTS, D)), _full((N_EXPERTS, D)), _full((N_EXPERTS, 1)),
            _full(wsgu.shape), _full((D_SHARED, D)),
        ],
        out_specs=[row_spec, row_spec, tok_spec, tok_spec, _full((N_EXPERTS, n_tm))],
        out_shape=[
            jax.ShapeDtypeStruct((N, D), F32),
            jax.ShapeDtypeStruct((N, D), BF16),
            jax.ShapeDtypeStruct((TOP_K, N), I32),
            jax.ShapeDtypeStruct((TOP_K, N), F32),
            jax.ShapeDtypeStruct((N_EXPERTS, n_tm), F32),
        ],
        compiler_params=_params("arbitrary"),
        name="mix",
    )(x.reshape(N, D), ypool.reshape(N, POOL_WIDTH), ymla.reshape(N, H * V_HEAD), w_out[0].astype(BF16),
      ln1_g, ln1_b, wr_hi, wr_lo, router_bias.reshape(N_EXPERTS, 1), wsgu, w_sh_down[0].astype(BF16))

    c8 = tile_cnt.T.astype(I32)
    l_start = jnp.cumsum(c8, axis=1) - c8
    expert_rows = jnp.sum(c8, axis=0)
    padded = (expert_rows + BM - 1) // BM * BM
    pad_end = jnp.cumsum(padded)
    g_start = (pad_end - padded)[None, :] + jnp.cumsum(c8, axis=0) - c8
    n_blocks = -(-(N * TOP_K + n_tm * N_EXPERTS * (SEG - 1) + N_EXPERTS * (BM - 1)) // BM)
    P = n_blocks * BM
    blk_row = jnp.arange(n_blocks, dtype=I32) * BM
    block_e = jnp.minimum(jnp.sum((pad_end[None, :] <= blk_row[:, None]).astype(I32), axis=1), N_EXPERTS - 1)
    n_used = (pad_end[-1] // BM).astype(I32).reshape(1)
    n_big = c8 // CHUNK
    big_tab, big_cnt = _chunk_table(n_big, jnp.zeros_like(c8), CHUNK, MAX_BIG, l_start, g_start)
    small_tab, small_cnt = _chunk_table((c8 - n_big * CHUNK) // SEG, n_big * CHUNK, SEG, MAX_SMALL,
                                        l_start, g_start)
    seg_tables = (big_tab, small_tab, big_cnt, small_cnt)
    fill_start = jnp.concatenate([pad_end - padded + expert_rows, pad_end[-1:]]).astype(I32)
    fill_rows = jnp.concatenate([padded - expert_rows, P - pad_end[-1:]]).astype(I32)

    buf = pl.pallas_call(
        _dispatch_kernel,
        grid_spec=pltpu.PrefetchScalarGridSpec(
            num_scalar_prefetch=6, grid=(n_tm,),
            in_specs=[pl.BlockSpec((TOP_K, TM), lambda i, *_: (0, i)),
                      pl.BlockSpec((TM, D), lambda i, *_: (i, 0))],
            out_specs=pl.BlockSpec(memory_space=pl.ANY),
            scratch_shapes=[pltpu.VMEM((2, TILE_ROWS, HALF), U32), pltpu.VMEM((TILE_ROWS, TM), BF16),
                            pltpu.VMEM((CHUNK, HALF), U32),
                            pltpu.SemaphoreType.DMA((2,)), pltpu.SemaphoreType.DMA]),
        out_shape=jax.ShapeDtypeStruct((P, HALF), U32),
        compiler_params=_params("arbitrary"),
        name="dispatch",
    )(*seg_tables, fill_start, fill_rows, lpos, hb)

    def used(i, nu):
        return jnp.minimum(i, nu[0] - 1)

    obuf = pl.pallas_call(
        _ffn_kernel,
        grid_spec=pltpu.PrefetchScalarGridSpec(
            num_scalar_prefetch=2, grid=(n_blocks,),
            in_specs=[pl.BlockSpec((BM, HALF), lambda i, be, nu: (used(i, nu), 0)),
                      pl.BlockSpec((1, D, D_EXPERT), lambda i, be, nu: (be[used(i, nu)], 0, 0)),
                      pl.BlockSpec((1, D, D_EXPERT), lambda i, be, nu: (be[used(i, nu)], 0, 0)),
                      pl.BlockSpec((1, D_EXPERT, D), lambda i, be, nu: (be[used(i, nu)], 0, 0))],
            out_specs=pl.BlockSpec((BM, HALF), lambda i, be, nu: (i, 0)),
            scratch_shapes=[pltpu.VMEM((D, 2 * D_EXPERT), BF16), pltpu.VMEM((D_EXPERT, D), BF16)]),
        out_shape=jax.ShapeDtypeStruct((P, HALF), U32),
        compiler_params=_params("arbitrary"),
        name="ffn",
    )(block_e, n_used, buf, w_gate[0], w_up[0], w_down[0])

    out = pl.pallas_call(
        _combine_kernel,
        grid_spec=pltpu.PrefetchScalarGridSpec(
            num_scalar_prefetch=4, grid=(n_tm,),
            in_specs=[pl.BlockSpec((TOP_K, TM), lambda i, *_: (0, i)),
                      pl.BlockSpec((TOP_K, TM), lambda i, *_: (0, i)),
                      pl.BlockSpec((TM, D), lambda i, *_: (i, 0)),
                      pl.BlockSpec((1, D), lambda i, *_: (0, 0)), pl.BlockSpec((1, D), lambda i, *_: (0, 0)),
                      pl.BlockSpec(memory_space=pl.ANY)],
            out_specs=pl.BlockSpec((TM, D), lambda i, *_: (i, 0)),
            scratch_shapes=[pltpu.VMEM((2, TILE_ROWS, HALF), U32), pltpu.VMEM((TILE_ROWS, TM), BF16),
                            pltpu.SemaphoreType.DMA((2,))]),
        out_shape=jax.ShapeDtypeStruct((N, D), F32),
        compiler_params=_params("arbitrary"),
        name="combine",
    )(*seg_tables, lpos, top_w, base, ln2_g, ln2_b, obuf)
    return out.reshape(B, S, D)
```

```python
import functools

import jax
import jax.numpy as jnp
from jax import lax
from jax.experimental import pallas as pl
from jax.experimental.pallas import tpu as pltpu

F32 = jnp.float32
BF16 = jnp.bfloat16
I32 = jnp.int32
U32 = jnp.uint32

D_MODEL = 1024
POOL_WINDOWS = (2, 4, 8, 16)
POOL_WIDTH = 512
POOL_GROUP = 128
MLA_HEADS = 8
QK_NOPE = 64
QK_ROPE = 32
QK_DIM = QK_NOPE + QK_ROPE
V_HEAD = 64
Q_LORA = 256
KV_LORA = 128
ROPE_THETA = 10000.0
HEAD_PAD = 128
HALO = 8

N_EXPERTS = 64
TOP_K = 8
N_GROUPS = 8
GROUP_SIZE = N_EXPERTS // N_GROUPS
TOPK_GROUPS = 4
D_EXPERT = 256
D_SHARED = 256
ROUTED_SCALE = 2.5

DEPTH = 1
ALPHA = (2 * DEPTH) ** 0.25
LN_EPS = 1e-5
RMS_EPS = 1e-6
LOG2_E = 1.4426950408889634

VMEM_LIMIT = 56 * 1024 * 1024

TS = 512
TQ = 512
TM = 256
TMX = 512
BM = 1024
HALF = D_MODEL // 2
SEG = 8
CHUNK = 32
TILE_ROWS = TM * TOP_K + N_EXPERTS * (SEG - 1)
TILE_ROWS += -TILE_ROWS % SEG
MAX_BIG = TILE_ROWS // CHUNK
MAX_SMALL = N_EXPERTS * (CHUNK // SEG - 1)
LOCAL_BITS = (TILE_ROWS // SEG - 1).bit_length()
UNROLL = 4
MAT_ROWS = 64
assert TILE_ROWS % MAT_ROWS == 0

NT_DIMS = (((1,), (1,)), ((), ()))
TN_DIMS = (((0,), (0,)), ((), ()))


def _pack_halves(lo, hi):
    ulo = lax.bitcast_convert_type(lo.astype(BF16).astype(F32), U32)
    uhi = lax.bitcast_convert_type(hi.astype(BF16).astype(F32), U32)
    return ulo | (uhi >> 16)


def _unpack_halves(p):
    lo = lax.bitcast_convert_type(p & jnp.uint32(0xFFFF0000), F32)
    hi = lax.bitcast_convert_type(p << 16, F32)
    return lo, hi


def _proj_kernel(x_ref, xl_ref, xr_ref, win_ref, wpool_ref, pscale_ref, qg_ref, wq_ref, kvg_ref,
                 wk_ref, wvt_ref, cos_ref, sin_ref, ypool_ref, q_ref, k_ref, vt_ref, uext_ref, *, seq_len):
    i = pl.program_id(1)
    n_i = pl.num_programs(1)
    ts = x_ref.shape[1]

    proj = jnp.dot(x_ref[0].astype(BF16), win_ref[...], preferred_element_type=F32)
    u = proj[:, :POOL_WIDTH]

    w_u = win_ref[:, :POOL_WIDTH]
    ul = jnp.dot(xl_ref[0].astype(BF16), w_u, preferred_element_type=F32)
    ur = jnp.dot(xr_ref[0].astype(BF16), w_u, preferred_element_type=F32)
    uext_ref[0:HALO, :] = jnp.where(i > 0, ul, 0.0)
    uext_ref[HALO:HALO + ts, :] = u
    uext_ref[HALO + ts:HALO + ts + HALO, :] = jnp.where(i < n_i - 1, ur, 0.0)

    pos = lax.broadcasted_iota(I32, (ts, 1), 0) + i * ts
    outs = []
    for g, w in enumerate(POOL_WINDOWS):
        left = w // 2
        right = w - 1 - left
        c0, c1 = g * POOL_GROUP, (g + 1) * POOL_GROUP
        acc = uext_ref[HALO - left:HALO - left + ts, c0:c1]
        for j in range(-left + 1, right + 1):
            acc = acc + uext_ref[HALO + j:HALO + j + ts, c0:c1]
        lo = jnp.maximum(pos - left, 0)
        hi = jnp.minimum(pos + right + 1, seq_len)
        pooled = acc / (hi - lo).astype(F32) - u[:, c0:c1]
        outs.append(jnp.dot(pooled.astype(BF16), wpool_ref[g], preferred_element_type=F32))
    ypool_ref[0] = (jnp.concatenate(outs, axis=1) * pscale_ref[...]).astype(BF16)

    cosf = cos_ref[...]
    sinf = sin_ref[...]

    ql = proj[:, POOL_WIDTH:POOL_WIDTH + Q_LORA]
    qn = ql * lax.rsqrt(jnp.mean(ql * ql, axis=-1, keepdims=True) + RMS_EPS) * qg_ref[...]
    qq = jnp.dot(qn.astype(BF16), wq_ref[...], preferred_element_type=F32)
    scale = QK_DIM ** -0.5 * LOG2_E
    hw = MLA_HEADS * HEAD_PAD
    for h in range(MLA_HEADS):
        qa = qq[:, h * HEAD_PAD:(h + 1) * HEAD_PAD]
        qb = qq[:, hw + h * HEAD_PAD:hw + (h + 1) * HEAD_PAD]
        q_ref[0, h] = ((qa * cosf + qb * sinf) * scale).astype(BF16)

    c_kv = POOL_WIDTH + Q_LORA
    kvl = proj[:, c_kv:c_kv + KV_LORA]
    kvn = kvl * lax.rsqrt(jnp.mean(kvl * kvl, axis=-1, keepdims=True) + RMS_EPS) * kvg_ref[...]
    kvb = kvn.astype(BF16)
    kn = jnp.dot(kvb, wk_ref[...], preferred_element_type=F32)
    c_r = c_kv + KV_LORA
    kr = proj[:, c_r:c_r + HEAD_PAD] * cosf + proj[:, c_r + HEAD_PAD:c_r + 2 * HEAD_PAD] * sinf
    for h in range(MLA_HEADS):
        k_ref[0, h] = (kn[:, h * HEAD_PAD:(h + 1) * HEAD_PAD] + kr).astype(BF16)
    vt_ref[0] = lax.dot_general(wvt_ref[...], kvb, NT_DIMS, preferred_element_type=F32).astype(BF16)


def _attn_kernel(q_ref, k_ref, vt_ref, o_ref):
    s_len = k_ref.shape[2]
    ones = jnp.ones((16, s_len), BF16)

    def scores(h):
        return lax.dot_general(k_ref[0, h], q_ref[0, h], NT_DIMS, preferred_element_type=F32)

    st_next = scores(0)
    outs = []
    for h in range(MLA_HEADS):
        st = st_next
        if h + 1 < MLA_HEADS:
            st_next = scores(h + 1)
        p = jnp.exp2(st - jnp.max(st, axis=0, keepdims=True)).astype(BF16)
        v_aug = jnp.concatenate([vt_ref[0, h * V_HEAD:(h + 1) * V_HEAD, :], ones], axis=0)
        ot = jnp.dot(v_aug, p, preferred_element_type=F32)
        outs.append(ot[:V_HEAD] / ot[V_HEAD:V_HEAD + 1])
    o_ref[0] = jnp.concatenate(outs, axis=0).T.astype(BF16)


def _first_argmax(vals, iota, sentinel):
    m = jnp.max(vals, axis=0, keepdims=True)
    idx = jnp.min(jnp.where(vals == m, iota, sentinel), axis=0, keepdims=True)
    return m, idx


def _mix_kernel(x_ref, yp_ref, ym_ref, wout_ref, g1_ref, b1_ref, wrh_ref, wrl_ref, rb_ref, wsgu_ref,
                wsd_ref, base_ref, hb_ref, lpos_ref, tw_ref, cnt_ref):
    i = pl.program_id(0)
    tm = x_ref.shape[0]

    @pl.when(i == 0)
    def _():
        cnt_ref[...] = jnp.zeros_like(cnt_ref)

    mixed = (jnp.dot(yp_ref[...], wout_ref[:POOL_WIDTH, :], preferred_element_type=F32)
             + jnp.dot(ym_ref[...], wout_ref[POOL_WIDTH:, :], preferred_element_type=F32))
    z = ALPHA * x_ref[...] + mixed
    mu = jnp.mean(z, axis=-1, keepdims=True)
    zc = z - mu
    var = jnp.mean(zc * zc, axis=-1, keepdims=True)
    h = zc * lax.rsqrt(var + LN_EPS) * g1_ref[...] + b1_ref[...]
    hb = h.astype(BF16)

    hb_ref[...] = hb

    h_lo = (h - hb.astype(F32)).astype(BF16)
    logits = (lax.dot_general(wrh_ref[...], hb, NT_DIMS, preferred_element_type=F32)
              + lax.dot_general(wrl_ref[...], hb, NT_DIMS, preferred_element_type=F32)
              + lax.dot_general(wrh_ref[...], h_lo, NT_DIMS, preferred_element_type=F32))
    scores = jax.nn.sigmoid(logits)
    biased = scores + rb_ref[...]

    neg_inf = jnp.float32(-jnp.inf)
    iota_g = lax.broadcasted_iota(I32, (GROUP_SIZE, tm), 0)
    grp_rows = []
    for g in range(N_GROUPS):
        bg = biased[g * GROUP_SIZE:(g + 1) * GROUP_SIZE, :]
        m1, i1 = _first_argmax(bg, iota_g, GROUP_SIZE)
        m2 = jnp.max(jnp.where(iota_g == i1, neg_inf, bg), axis=0, keepdims=True)
        grp_rows.append(m1 + m2)
    grp_score = jnp.concatenate(grp_rows, axis=0)

    iota_n = lax.broadcasted_iota(I32, (N_GROUPS, tm), 0)
    grp_sel = jnp.zeros((N_GROUPS, tm), jnp.bool_)
    for _ in range(TOPK_GROUPS):
        _, gi = _first_argmax(grp_score, iota_n, N_GROUPS)
        hit = iota_n == gi
        grp_sel = grp_sel | hit
        grp_score = jnp.where(hit, neg_inf, grp_score)

    cand = jnp.concatenate(
        [jnp.where(grp_sel[g:g + 1, :], biased[g * GROUP_SIZE:(g + 1) * GROUP_SIZE, :], neg_inf)
         for g in range(N_GROUPS)], axis=0)

    iota_e = lax.broadcasted_iota(I32, (N_EXPERTS, tm), 0)
    sel = jnp.zeros((N_EXPERTS, tm), jnp.bool_)
    e_rows, w_rows = [], []
    for _ in range(TOP_K):
        _, ei = _first_argmax(cand, iota_e, N_EXPERTS)
        hit = iota_e == ei
        e_rows.append(ei)
        w_rows.append(jnp.sum(jnp.where(hit, scores, 0.0), axis=0, keepdims=True))
        sel = sel | hit
        cand = jnp.where(hit, neg_inf, cand)
    top_s = jnp.concatenate(w_rows, axis=0)
    top_w = top_s / jnp.sum(top_s, axis=0, keepdims=True) * ROUTED_SCALE

    sel_b = sel.astype(BF16)
    r_i = lax.broadcasted_iota(I32, (TM, TM), 0)
    c_i = lax.broadcasted_iota(I32, (TM, TM), 1)
    before = (r_i < c_i).astype(BF16)
    e_r = lax.broadcasted_iota(I32, (N_EXPERTS, N_EXPERTS), 0)
    e_c = lax.broadcasted_iota(I32, (N_EXPERTS, N_EXPERTS), 1)
    earlier = (e_c < e_r).astype(BF16)
    lane_t = lax.broadcasted_iota(I32, cnt_ref.shape, 1)
    cnt = cnt_ref[...]
    pos_parts = []
    for j in range(tm // TM):
        sel_j = sel_b[:, j * TM:(j + 1) * TM]
        rank = jnp.dot(sel_j, before, preferred_element_type=F32)
        count = jnp.sum(sel_j.astype(F32), axis=1, keepdims=True)
        granules = jnp.ceil(count * (1.0 / SEG))
        g_b = jnp.broadcast_to(granules, (N_EXPERTS, 128)).astype(BF16)
        seg_start = jnp.dot(earlier, g_b, preferred_element_type=F32)[:, 0:1] * SEG
        pos_parts.append(seg_start + rank)
        cnt = jnp.where(lane_t == i * (tm // TM) + j, granules * SEG, cnt)
    cnt_ref[...] = cnt
    pos = jnp.concatenate(pos_parts, axis=1)
    p_rows = [jnp.sum(jnp.where(iota_e == e_rows[kk], pos, 0.0), axis=0, keepdims=True)
              for kk in range(TOP_K)]
    lpos_ref[...] = jnp.concatenate(p_rows, axis=0).astype(I32)
    tw_ref[...] = top_w

    gu = jnp.dot(hb, wsgu_ref[...], preferred_element_type=F32)
    act = jax.nn.silu(gu[:, :D_SHARED]) * gu[:, D_SHARED:]
    shared = jnp.dot(act.astype(BF16), wsd_ref[...], preferred_element_type=F32)
    base_ref[...] = ALPHA * h + shared


def _position_matrix(lpos, values, out_ref):
    tm = lpos.shape[1]
    lp = lpos.astype(jnp.int16)
    vals = None if values is None else values.astype(BF16)
    iota = lax.broadcasted_iota(jnp.int16, (MAT_ROWS, tm), 0)
    for c in range(TILE_ROWS // MAT_ROWS):
        acc = jnp.zeros((MAT_ROWS, tm), BF16)
        for kk in range(TOP_K):
            hit = iota == lp[kk:kk + 1, :] - jnp.int16(c * MAT_ROWS)
            acc = jnp.where(hit, jnp.bfloat16(1.0) if vals is None else vals[kk:kk + 1, :], acc)
        out_ref[c * MAT_ROWS:(c + 1) * MAT_ROWS, :] = acc


def _for_each_tile_copy(tabs, tile, make_copy, act):
    big_ref, small_ref, nbig_ref, nsmall_ref = tabs
    for tab_ref, n_ref, cap, rows in ((big_ref, nbig_ref, MAX_BIG, CHUNK), (small_ref, nsmall_ref, MAX_SMALL, SEG)):
        def one(j, tab_ref=tab_ref, cap=cap, rows=rows):
            p = tab_ref[tile * cap + j]
            lrow = pl.multiple_of((p & (2 ** LOCAL_BITS - 1)) * SEG, SEG)
            grow = pl.multiple_of(lax.shift_right_logical(p, LOCAL_BITS) * SEG, SEG)
            act(make_copy(lrow, grow, rows))

        n = n_ref[tile]
        n_groups = lax.shift_right_logical(n, UNROLL.bit_length() - 1)

        def group(q, c, one=one):
            for u in range(UNROLL):
                one(q * UNROLL + u)
            return c

        def single(q, c, one=one, n_groups=n_groups):
            one(n_groups * UNROLL + q)
            return c

        lax.fori_loop(0, n_groups, group, 0)
        lax.fori_loop(0, n - n_groups * UNROLL, single, 0)


def _dispatch_kernel(big_ref, small_ref, nbig_ref, nsmall_ref, fs_ref, fn_ref, lpos_ref, hb_ref, buf_ref,
                     sorted_ref, mat_ref, zero_ref, sem, zsem):
    tabs = (big_ref, small_ref, nbig_ref, nsmall_ref)
    i = pl.program_id(0)
    n_i = pl.num_programs(0)
    slot = lax.rem(i, 2)
    _position_matrix(lpos_ref[...], None, mat_ref)
    s = jnp.dot(mat_ref[...], hb_ref[...], preferred_element_type=F32)
    sorted_ref[slot] = (lax.bitcast_convert_type(s[:, :HALF], U32)
                        | (lax.bitcast_convert_type(s[:, HALF:], U32) >> 16))

    def copies_from(sl):
        def make_copy(lrow, grow, rows):
            return pltpu.make_async_copy(sorted_ref.at[sl, pl.ds(lrow, rows)], buf_ref.at[pl.ds(grow, rows)],
                                         sem.at[sl])
        return make_copy

    _for_each_tile_copy(tabs, i, copies_from(slot), lambda cp: cp.start())

    @pl.when(i > 0)
    def _():
        _for_each_tile_copy(tabs, i - 1, copies_from(1 - slot), lambda cp: cp.wait())

    @pl.when(i == n_i - 1)
    def _():
        _for_each_tile_copy(tabs, i, copies_from(slot), lambda cp: cp.wait())
        zero_ref[...] = jnp.zeros_like(zero_ref)

        def zero_copy(grow, rows):
            return pltpu.make_async_copy(zero_ref.at[pl.ds(0, rows)], buf_ref.at[pl.ds(grow, rows)], zsem)

        def fill(act):
            def gap(e, carry):
                gs, n = fs_ref[e], fn_ref[e]
                n_big = lax.shift_right_logical(n, CHUNK.bit_length() - 1)
                n_small = lax.shift_right_logical(n - n_big * CHUNK, SEG.bit_length() - 1)

                def big(q, c):
                    act(zero_copy(pl.multiple_of(gs + q * CHUNK, SEG), CHUNK))
                    return c

                def small(q, c):
                    act(zero_copy(pl.multiple_of(gs + n_big * CHUNK + q * SEG, SEG), SEG))
                    return c

                lax.fori_loop(0, n_big, big, 0)
                lax.fori_loop(0, n_small, small, 0)
                return carry

            lax.fori_loop(0, fs_ref.shape[0], gap, 0)

        fill(lambda cp: cp.start())
        fill(lambda cp: cp.wait())


def _ffn_kernel(be_ref, nu_ref, x_ref, wg_ref, wu_ref, wd_ref, o_ref, wgu_s, wd_s):
    i = pl.program_id(0)
    n_used = nu_ref[0]

    @pl.when((i == 0) | ((i < n_used) & (be_ref[i] != be_ref[jnp.maximum(i - 1, 0)])))
    def _():
        wgu_s[:, :D_EXPERT] = wg_ref[0].astype(BF16)
        wgu_s[:, D_EXPERT:] = wu_ref[0].astype(BF16)
        wd_s[...] = wd_ref[0].astype(BF16)

    @pl.when(i < n_used)
    def _():
        lo, hi = _unpack_halves(x_ref[...])
        x = jnp.concatenate([lo.astype(BF16), hi.astype(BF16)], axis=1)
        gu = jnp.dot(x, wgu_s[...], preferred_element_type=F32)
        act = jax.nn.silu(gu[:, :D_EXPERT]) * gu[:, D_EXPERT:]
        y = jnp.dot(act.astype(BF16), wd_s[...], preferred_element_type=F32)
        o_ref[...] = _pack_halves(y[:, :HALF], y[:, HALF:])

    @pl.when(i >= n_used)
    def _():
        o_ref[...] = jnp.zeros_like(o_ref)


def _combine_kernel(big_ref, small_ref, nbig_ref, nsmall_ref, lpos_ref, tw_ref, base_ref, g2_ref, b2_ref,
                    obuf_ref, out_ref, sorted_ref, mat_ref, sem):
    tabs = (big_ref, small_ref, nbig_ref, nsmall_ref)
    i = pl.program_id(0)
    n_i = pl.num_programs(0)
    slot = lax.rem(i, 2)

    def copies_into(sl):
        def make_copy(lrow, grow, rows):
            return pltpu.make_async_copy(obuf_ref.at[pl.ds(grow, rows)], sorted_ref.at[sl, pl.ds(lrow, rows)],
                                         sem.at[sl])
        return make_copy

    @pl.when(i == 0)
    def _():
        sorted_ref[...] = jnp.zeros_like(sorted_ref)
        _for_each_tile_copy(tabs, i, copies_into(slot), lambda cp: cp.start())

    @pl.when(i + 1 < n_i)
    def _():
        _for_each_tile_copy(tabs, i + 1, copies_into(1 - slot), lambda cp: cp.start())

    _position_matrix(lpos_ref[...], tw_ref[...], mat_ref)

    _for_each_tile_copy(tabs, i, copies_into(slot), lambda cp: cp.wait())
    lo, hi = _unpack_halves(sorted_ref[slot])
    wb = mat_ref[...]
    routed = jnp.concatenate([lax.dot_general(wb, lo.astype(BF16), TN_DIMS, preferred_element_type=F32),
                              lax.dot_general(wb, hi.astype(BF16), TN_DIMS, preferred_element_type=F32)],
                             axis=1)
    z = base_ref[...] + routed
    mu = jnp.mean(z, axis=-1, keepdims=True)
    zc = z - mu
    var = jnp.mean(zc * zc, axis=-1, keepdims=True)
    out_ref[...] = zc * lax.rsqrt(var + LN_EPS) * g2_ref[...] + b2_ref[...]


def _full(shape):
    nd = len(shape)
    return pl.BlockSpec(shape, lambda *_: (0,) * nd)


def _chunk_table(count, first_row, rows, cap, l_start, g_start):
    end = jnp.cumsum(count, axis=1)
    q = jnp.arange(cap, dtype=I32)
    expert = jnp.minimum(jnp.sum((end[:, None, :] <= q[None, :, None]).astype(I32), axis=2), N_EXPERTS - 1)
    hit = expert[:, :, None] == jnp.arange(N_EXPERTS, dtype=I32)[None, None, :]

    def pick(a):
        return jnp.sum(jnp.where(hit, a[:, None, :], 0), axis=2)

    off = pick(first_row) + (q[None, :] - pick(end - count)) * rows
    entry = ((pick(g_start) + off) // SEG) * 2 ** LOCAL_BITS + (pick(l_start) + off) // SEG
    return entry.reshape(-1).astype(I32), end[:, -1].astype(I32)


def _params(*sem):
    return pltpu.CompilerParams(dimension_semantics=sem, vmem_limit_bytes=VMEM_LIMIT)


def kernel(x, w_in, w_pool, pool_scale, q_norm_g, w_q_up, kv_norm_g, w_kv_up, w_out, ln1_g, ln1_b,
           w_router, router_bias, w_gate, w_up, w_down, w_sh_gate, w_sh_up, w_sh_down, ln2_g, ln2_b):
    B, S, D = x.shape
    assert D == D_MODEL and S % TS == 0 and S % TQ == 0 and (B * S) % TMX == 0 and TMX % TM == 0
    assert TS % HALO == 0 and TILE_ROWS < 2 ** 15
    N = B * S
    H = MLA_HEADS
    hw = H * HEAD_PAD

    wi = w_in[0]
    c_r = POOL_WIDTH + Q_LORA + KV_LORA
    rope = wi[:, c_r:c_r + QK_ROPE]
    half = QK_ROPE // 2
    zc64 = jnp.zeros((D, QK_NOPE), F32)
    zc32 = jnp.zeros((D, HEAD_PAD - QK_DIM), F32)
    rope_a = jnp.concatenate([zc64, rope, zc32], axis=1)
    rope_b = jnp.concatenate([zc64, -rope[:, half:], rope[:, :half], zc32], axis=1)
    win_p = jnp.concatenate([wi[:, :c_r], rope_a, rope_b], axis=1).astype(BF16)

    wq = w_q_up[0].reshape(Q_LORA, H, QK_DIM)
    zq = jnp.zeros((Q_LORA, H, HEAD_PAD - QK_DIM), F32)
    wq_a = jnp.concatenate([wq, zq], axis=2).reshape(Q_LORA, hw)
    wq_b = jnp.concatenate([jnp.zeros((Q_LORA, H, QK_NOPE), F32), -wq[:, :, QK_NOPE + half:],
                            wq[:, :, QK_NOPE:QK_NOPE + half], zq], axis=2).reshape(Q_LORA, hw)
    wq_p = jnp.concatenate([wq_a, wq_b], axis=1).astype(BF16)

    wkv = w_kv_up[0].reshape(KV_LORA, H, QK_NOPE + V_HEAD)
    wk = jnp.concatenate([wkv[:, :, :QK_NOPE], jnp.zeros((KV_LORA, H, HEAD_PAD - QK_NOPE), F32)],
                         axis=2).reshape(KV_LORA, hw)
    wk_p = wk.astype(BF16)
    wvt_p = wkv[:, :, QK_NOPE:].reshape(KV_LORA, H * V_HEAD).T.astype(BF16)

    pos = jnp.arange(S, dtype=F32)
    inv_freq = ROPE_THETA ** (-jnp.arange(0, QK_ROPE, 2, dtype=F32) / QK_ROPE)
    ang = pos[:, None] * inv_freq[None, :]
    cosv, sinv = jnp.cos(ang), jnp.sin(ang)
    cos_t = jnp.concatenate([jnp.ones((S, QK_NOPE), F32), cosv, cosv,
                             jnp.zeros((S, HEAD_PAD - QK_DIM), F32)], axis=1)
    sin_t = jnp.concatenate([jnp.zeros((S, QK_NOPE), F32), sinv, sinv,
                             jnp.zeros((S, HEAD_PAD - QK_DIM), F32)], axis=1)

    n_ts = S // TS
    hb = TS // HALO
    ypool, q, k, vt = pl.pallas_call(
        functools.partial(_proj_kernel, seq_len=S),
        grid=(B, n_ts),
        in_specs=[
            pl.BlockSpec((1, TS, D), lambda b, i: (b, i, 0)),
            pl.BlockSpec((1, HALO, D), lambda b, i: (b, jnp.maximum(i * hb - 1, 0), 0)),
            pl.BlockSpec((1, HALO, D), lambda b, i: (b, jnp.minimum((i + 1) * hb, S // HALO - 1), 0)),
            _full(win_p.shape), _full((len(POOL_WINDOWS), POOL_GROUP, POOL_GROUP)),
            _full((1, POOL_WIDTH)), _full((1, Q_LORA)), _full(wq_p.shape), _full((1, KV_LORA)),
            _full(wk_p.shape), _full(wvt_p.shape),
            pl.BlockSpec((TS, HEAD_PAD), lambda b, i: (i, 0)),
            pl.BlockSpec((TS, HEAD_PAD), lambda b, i: (i, 0)),
        ],
        out_specs=[
            pl.BlockSpec((1, TS, POOL_WIDTH), lambda b, i: (b, i, 0)),
            pl.BlockSpec((1, H, TS, HEAD_PAD), lambda b, i: (b, 0, i, 0)),
            pl.BlockSpec((1, H, TS, HEAD_PAD), lambda b, i: (b, 0, i, 0)),
            pl.BlockSpec((1, H * V_HEAD, TS), lambda b, i: (b, 0, i)),
        ],
        out_shape=[
            jax.ShapeDtypeStruct((B, S, POOL_WIDTH), BF16),
            jax.ShapeDtypeStruct((B, H, S, HEAD_PAD), BF16),
            jax.ShapeDtypeStruct((B, H, S, HEAD_PAD), BF16),
            jax.ShapeDtypeStruct((B, H * V_HEAD, S), BF16),
        ],
        scratch_shapes=[pltpu.VMEM((TS + 2 * HALO, POOL_WIDTH), F32)],
        compiler_params=_params("parallel", "arbitrary"),
        name="proj",
    )(x, x, x, win_p, w_pool[0].astype(BF16), pool_scale, q_norm_g, wq_p, kv_norm_g, wk_p, wvt_p,
      cos_t, sin_t)

    ymla = pl.pallas_call(
        _attn_kernel,
        grid=(B, S // TQ),
        in_specs=[
            pl.BlockSpec((1, H, TQ, HEAD_PAD), lambda b, i: (b, 0, i, 0)),
            pl.BlockSpec((1, H, S, HEAD_PAD), lambda b, i: (b, 0, 0, 0)),
            pl.BlockSpec((1, H * V_HEAD, S), lambda b, i: (b, 0, 0)),
        ],
        out_specs=pl.BlockSpec((1, TQ, H * V_HEAD), lambda b, i: (b, i, 0)),
        out_shape=jax.ShapeDtypeStruct((B, S, H * V_HEAD), BF16),
        compiler_params=_params("parallel", "arbitrary"),
        name="attn",
    )(q, k, vt)

    wr_t = w_router[0].T
    wr_hi = wr_t.astype(BF16)
    wr_lo = (wr_t - wr_hi.astype(F32)).astype(BF16)
    wsgu = jnp.concatenate([w_sh_gate[0], w_sh_up[0]], axis=1).astype(BF16)
    n_tm = N // TM
    row_spec = pl.BlockSpec((TMX, D), lambda i: (i, 0))
    half_spec = pl.BlockSpec((TMX, POOL_WIDTH), lambda i: (i, 0))
    tok_spec = pl.BlockSpec((TOP_K, TMX), lambda i: (0, i))
    base, hb, lpos, top_w, tile_cnt = pl.pallas_call(
        _mix_kernel,
        grid=(N // TMX,),
        in_specs=[
            row_spec, half_spec, half_spec, _full((D, D)), _full((1, D)), _full((1, D)),
            _full((N_EXPERTS, D)), _full((N_EXPERTS, D)), _full((N_EXPERTS, 1)),
            _full(wsgu.shape), _full((D_SHARED, D)),
        ],
        out_specs=[row_spec, row_spec, tok_spec, tok_spec, _full((N_EXPERTS, n_tm))],
        out_shape=[
            jax.ShapeDtypeStruct((N, D), F32),
            jax.ShapeDtypeStruct((N, D), BF16),
            jax.ShapeDtypeStruct((TOP_K, N), I32),
            jax.ShapeDtypeStruct((TOP_K, N), F32),
            jax.ShapeDtypeStruct((N_EXPERTS, n_tm), F32),
        ],
        compiler_params=_params("arbitrary"),
        name="mix",
    )(x.reshape(N, D), ypool.reshape(N, POOL_WIDTH), ymla.reshape(N, H * V_HEAD), w_out[0].astype(BF16),
      ln1_g, ln1_b, wr_hi, wr_lo, router_bias.reshape(N_EXPERTS, 1), wsgu, w_sh_down[0].astype(BF16))

    c8 = tile_cnt.T.astype(I32)
    l_start = jnp.cumsum(c8, axis=1) - c8
    expert_rows = jnp.sum(c8, axis=0)
    padded = (expert_rows + BM - 1) // BM * BM
    pad_end = jnp.cumsum(padded)
    g_start = (pad_end - padded)[None, :] + jnp.cumsum(c8, axis=0) - c8
    n_blocks = -(-(N * TOP_K + n_tm * N_EXPERTS * (SEG - 1) + N_EXPERTS * (BM - 1)) // BM)
    P = n_blocks * BM
    blk_row = jnp.arange(n_blocks, dtype=I32) * BM
    block_e = jnp.minimum(jnp.sum((pad_end[None, :] <= blk_row[:, None]).astype(I32), axis=1), N_EXPERTS - 1)
    n_used = (pad_end[-1] // BM).astype(I32).reshape(1)
    n_big = c8 // CHUNK
    big_tab, big_cnt = _chunk_table(n_big, jnp.zeros_like(c8), CHUNK, MAX_BIG, l_start, g_start)
    small_tab, small_cnt = _chunk_table((c8 - n_big * CHUNK) // SEG, n_big * CHUNK, SEG, MAX_SMALL,
                                        l_start, g_start)
    seg_tables = (big_tab, small_tab, big_cnt, small_cnt)
    fill_start = jnp.concatenate([pad_end - padded + expert_rows, pad_end[-1:]]).astype(I32)
    fill_rows = jnp.concatenate([padded - expert_rows, P - pad_end[-1:]]).astype(I32)

    buf = pl.pallas_call(
        _dispatch_kernel,
        grid_spec=pltpu.PrefetchScalarGridSpec(
            num_scalar_prefetch=6, grid=(n_tm,),
            in_specs=[pl.BlockSpec((TOP_K, TM), lambda i, *_: (0, i)),
                      pl.BlockSpec((TM, D), lambda i, *_: (i, 0))],
            out_specs=pl.BlockSpec(memory_space=pl.ANY),
            scratch_shapes=[pltpu.VMEM((2, TILE_ROWS, HALF), U32), pltpu.VMEM((TILE_ROWS, TM), BF16),
                            pltpu.VMEM((CHUNK, HALF), U32),
                            pltpu.SemaphoreType.DMA((2,)), pltpu.SemaphoreType.DMA]),
        out_shape=jax.ShapeDtypeStruct((P, HALF), U32),
        compiler_params=_params("arbitrary"),
        name="dispatch",
    )(*seg_tables, fill_start, fill_rows, lpos, hb)

    def used(i, nu):
        return jnp.minimum(i, nu[0] - 1)

    obuf = pl.pallas_call(
        _ffn_kernel,
        grid_spec=pltpu.PrefetchScalarGridSpec(
            num_scalar_prefetch=2, grid=(n_blocks,),
            in_specs=[pl.BlockSpec((BM, HALF), lambda i, be, nu: (used(i, nu), 0)),
                      pl.BlockSpec((1, D, D_EXPERT), lambda i, be, nu: (be[used(i, nu)], 0, 0)),
                      pl.BlockSpec((1, D, D_EXPERT), lambda i, be, nu: (be[used(i, nu)], 0, 0)),
                      pl.BlockSpec((1, D_EXPERT, D), lambda i, be, nu: (be[used(i, nu)], 0, 0))],
            out_specs=pl.BlockSpec((BM, HALF), lambda i, be, nu: (i, 0)),
            scratch_shapes=[pltpu.VMEM((D, 2 * D_EXPERT), BF16), pltpu.VMEM((D_EXPERT, D), BF16)]),
        out_shape=jax.ShapeDtypeStruct((P, HALF), U32),
        compiler_params=_params("arbitrary"),
        name="ffn",
    )(block_e, n_used, buf, w_gate[0], w_up[0], w_down[0])

    out = pl.pallas_call(
        _combine_kernel,
        grid_spec=pltpu.PrefetchScalarGridSpec(
            num_scalar_prefetch=4, grid=(n_tm,),
            in_specs=[pl.BlockSpec((TOP_K, TM), lambda i, *_: (0, i)),
                      pl.BlockSpec((TOP_K, TM), lambda i, *_: (0, i)),
                      pl.BlockSpec((TM, D), lambda i, *_: (i, 0)),
                      pl.BlockSpec((1, D), lambda i, *_: (0, 0)), pl.BlockSpec((1, D), lambda i, *_: (0, 0)),
                      pl.BlockSpec(memory_space=pl.ANY)],
            out_specs=pl.BlockSpec((TM, D), lambda i, *_: (i, 0)),
            scratch_shapes=[pltpu.VMEM((2, TILE_ROWS, HALF), U32), pltpu.VMEM((TILE_ROWS, TM), BF16),
                            pltpu.SemaphoreType.DMA((2,))]),
        out_shape=jax.ShapeDtypeStruct((N, D), F32),
        compiler_params=_params("arbitrary"),
        name="combine",
    )(*seg_tables, lpos, top_w, base, ln2_g, ln2_b, obuf)
    return out.reshape(B, S, D)
```

```python
import functools

import jax
import jax.numpy as jnp
from jax import lax
from jax.experimental import pallas as pl
from jax.experimental.pallas import tpu as pltpu

F32 = jnp.float32
BF16 = jnp.bfloat16
I32 = jnp.int32
U32 = jnp.uint32

D_MODEL = 1024
POOL_WINDOWS = (2, 4, 8, 16)
POOL_WIDTH = 512
POOL_GROUP = 128
MLA_HEADS = 8
QK_NOPE = 64
QK_ROPE = 32
QK_DIM = QK_NOPE + QK_ROPE
V_HEAD = 64
Q_LORA = 256
KV_LORA = 128
ROPE_THETA = 10000.0
HEAD_PAD = 128
HALO = 8

N_EXPERTS = 64
TOP_K = 8
N_GROUPS = 8
GROUP_SIZE = N_EXPERTS // N_GROUPS
TOPK_GROUPS = 4
D_EXPERT = 256
D_SHARED = 256
ROUTED_SCALE = 2.5

DEPTH = 1
ALPHA = (2 * DEPTH) ** 0.25
LN_EPS = 1e-5
RMS_EPS = 1e-6
LOG2_E = 1.4426950408889634

VMEM_LIMIT = 56 * 1024 * 1024

TS = 512
TQ = 512
TM = 256
TMX = 512
BM = 512
HALF = D_MODEL // 2
SEG = 8
CHUNK = 32
TILE_ROWS = TM * TOP_K + N_EXPERTS * (SEG - 1)
TILE_ROWS += -TILE_ROWS % SEG
MAX_BIG = TILE_ROWS // CHUNK
MAX_SMALL = N_EXPERTS * (CHUNK // SEG - 1)
LOCAL_BITS = (TILE_ROWS // SEG - 1).bit_length()
UNROLL = 4
MAT_ROWS = 64
assert TILE_ROWS % MAT_ROWS == 0

NT_DIMS = (((1,), (1,)), ((), ()))
TN_DIMS = (((0,), (0,)), ((), ()))


def _pack_halves(lo, hi):
    ulo = lax.bitcast_convert_type(lo.astype(BF16).astype(F32), U32)
    uhi = lax.bitcast_convert_type(hi.astype(BF16).astype(F32), U32)
    return ulo | (uhi >> 16)


def _unpack_halves(p):
    lo = lax.bitcast_convert_type(p & jnp.uint32(0xFFFF0000), F32)
    hi = lax.bitcast_convert_type(p << 16, F32)
    return lo, hi


def _proj_kernel(x_ref, xl_ref, xr_ref, win_ref, wpool_ref, pscale_ref, qg_ref, wq_ref, kvg_ref,
                 wk_ref, wvt_ref, cos_ref, sin_ref, ypool_ref, q_ref, k_ref, vt_ref, uext_ref, *, seq_len):
    i = pl.program_id(1)
    n_i = pl.num_programs(1)
    ts = x_ref.shape[1]

    proj = jnp.dot(x_ref[0].astype(BF16), win_ref[...], preferred_element_type=F32)
    u = proj[:, :POOL_WIDTH]

    w_u = win_ref[:, :POOL_WIDTH]
    ul = jnp.dot(xl_ref[0].astype(BF16), w_u, preferred_element_type=F32)
    ur = jnp.dot(xr_ref[0].astype(BF16), w_u, preferred_element_type=F32)
    uext_ref[0:HALO, :] = jnp.where(i > 0, ul, 0.0)
    uext_ref[HALO:HALO + ts, :] = u
    uext_ref[HALO + ts:HALO + ts + HALO, :] = jnp.where(i < n_i - 1, ur, 0.0)

    pos = lax.broadcasted_iota(I32, (ts, 1), 0) + i * ts
    outs = []
    for g, w in enumerate(POOL_WINDOWS):
        left = w // 2
        right = w - 1 - left
        c0, c1 = g * POOL_GROUP, (g + 1) * POOL_GROUP
        acc = uext_ref[HALO - left:HALO - left + ts, c0:c1]
        for j in range(-left + 1, right + 1):
            acc = acc + uext_ref[HALO + j:HALO + j + ts, c0:c1]
        lo = jnp.maximum(pos - left, 0)
        hi = jnp.minimum(pos + right + 1, seq_len)
        pooled = acc / (hi - lo).astype(F32) - u[:, c0:c1]
        outs.append(jnp.dot(pooled.astype(BF16), wpool_ref[g], preferred_element_type=F32))
    ypool_ref[0] = (jnp.concatenate(outs, axis=1) * pscale_ref[...]).astype(BF16)

    cosf = cos_ref[...]
    sinf = sin_ref[...]

    ql = proj[:, POOL_WIDTH:POOL_WIDTH + Q_LORA]
    qn = ql * lax.rsqrt(jnp.mean(ql * ql, axis=-1, keepdims=True) + RMS_EPS) * qg_ref[...]
    qq = jnp.dot(qn.astype(BF16), wq_ref[...], preferred_element_type=F32)
    scale = QK_DIM ** -0.5 * LOG2_E
    hw = MLA_HEADS * HEAD_PAD
    for h in range(MLA_HEADS):
        qa = qq[:, h * HEAD_PAD:(h + 1) * HEAD_PAD]
        qb = qq[:, hw + h * HEAD_PAD:hw + (h + 1) * HEAD_PAD]
        q_ref[0, h] = ((qa * cosf + qb * sinf) * scale).astype(BF16)

    c_kv = POOL_WIDTH + Q_LORA
    kvl = proj[:, c_kv:c_kv + KV_LORA]
    kvn = kvl * lax.rsqrt(jnp.mean(kvl * kvl, axis=-1, keepdims=True) + RMS_EPS) * kvg_ref[...]
    kvb = kvn.astype(BF16)
    kn = jnp.dot(kvb, wk_ref[...], preferred_element_type=F32)
    c_r = c_kv + KV_LORA
    kr = proj[:, c_r:c_r + HEAD_PAD] * cosf + proj[:, c_r + HEAD_PAD:c_r + 2 * HEAD_PAD] * sinf
    for h in range(MLA_HEADS):
        k_ref[0, h] = (kn[:, h * HEAD_PAD:(h + 1) * HEAD_PAD] + kr).astype(BF16)
    vt_ref[0] = lax.dot_general(wvt_ref[...], kvb, NT_DIMS, preferred_element_type=F32).astype(BF16)


def _attn_kernel(q_ref, k_ref, vt_ref, o_ref):
    s_len = k_ref.shape[2]
    ones = jnp.ones((16, s_len), BF16)

    def scores(h):
        return lax.dot_general(k_ref[0, h], q_ref[0, h], NT_DIMS, preferred_element_type=F32)

    st_next = scores(0)
    outs = []
    for h in range(MLA_HEADS):
        st = st_next
        if h + 1 < MLA_HEADS:
            st_next = scores(h + 1)
        p = jnp.exp2(st - jnp.max(st, axis=0, keepdims=True)).astype(BF16)
        v_aug = jnp.concatenate([vt_ref[0, h * V_HEAD:(h + 1) * V_HEAD, :], ones], axis=0)
        ot = jnp.dot(v_aug, p, preferred_element_type=F32)
        outs.append(ot[:V_HEAD] / ot[V_HEAD:V_HEAD + 1])
    o_ref[0] = jnp.concatenate(outs, axis=0).T.astype(BF16)


def _first_argmax(vals, iota, sentinel):
    m = jnp.max(vals, axis=0, keepdims=True)
    idx = jnp.min(jnp.where(vals == m, iota, sentinel), axis=0, keepdims=True)
    return m, idx


def _mix_kernel(x_ref, yp_ref, ym_ref, wout_ref, g1_ref, b1_ref, wrh_ref, wrl_ref, rb_ref, wsgu_ref,
                wsd_ref, base_ref, hb_ref, lpos_ref, tw_ref, cnt_ref):
    i = pl.program_id(0)
    tm = x_ref.shape[0]

    @pl.when(i == 0)
    def _():
        cnt_ref[...] = jnp.zeros_like(cnt_ref)

    mixed = (jnp.dot(yp_ref[...], wout_ref[:POOL_WIDTH, :], preferred_element_type=F32)
             + jnp.dot(ym_ref[...], wout_ref[POOL_WIDTH:, :], preferred_element_type=F32))
    z = ALPHA * x_ref[...] + mixed
    mu = jnp.mean(z, axis=-1, keepdims=True)
    zc = z - mu
    var = jnp.mean(zc * zc, axis=-1, keepdims=True)
    h = zc * lax.rsqrt(var + LN_EPS) * g1_ref[...] + b1_ref[...]
    hb = h.astype(BF16)

    hb_ref[...] = hb

    h_lo = (h - hb.astype(F32)).astype(BF16)
    logits = (lax.dot_general(wrh_ref[...], hb, NT_DIMS, preferred_element_type=F32)
              + lax.dot_general(wrl_ref[...], hb, NT_DIMS, preferred_element_type=F32)
              + lax.dot_general(wrh_ref[...], h_lo, NT_DIMS, preferred_element_type=F32))
    scores = jax.nn.sigmoid(logits)
    biased = scores + rb_ref[...]

    neg_inf = jnp.float32(-jnp.inf)
    iota_g = lax.broadcasted_iota(I32, (GROUP_SIZE, tm), 0)
    grp_rows = []
    for g in range(N_GROUPS):
        bg = biased[g * GROUP_SIZE:(g + 1) * GROUP_SIZE, :]
        m1, i1 = _first_argmax(bg, iota_g, GROUP_SIZE)
        m2 = jnp.max(jnp.where(iota_g == i1, neg_inf, bg), axis=0, keepdims=True)
        grp_rows.append(m1 + m2)
    grp_score = jnp.concatenate(grp_rows, axis=0)

    iota_n = lax.broadcasted_iota(I32, (N_GROUPS, tm), 0)
    grp_sel = jnp.zeros((N_GROUPS, tm), jnp.bool_)
    for _ in range(TOPK_GROUPS):
        _, gi = _first_argmax(grp_score, iota_n, N_GROUPS)
        hit = iota_n == gi
        grp_sel = grp_sel | hit
        grp_score = jnp.where(hit, neg_inf, grp_score)

    cand = jnp.concatenate(
        [jnp.where(grp_sel[g:g + 1, :], biased[g * GROUP_SIZE:(g + 1) * GROUP_SIZE, :], neg_inf)
         for g in range(N_GROUPS)], axis=0)

    iota_e = lax.broadcasted_iota(I32, (N_EXPERTS, tm), 0)
    sel = jnp.zeros((N_EXPERTS, tm), jnp.bool_)
    e_rows, w_rows = [], []
    for _ in range(TOP_K):
        _, ei = _first_argmax(cand, iota_e, N_EXPERTS)
        hit = iota_e == ei
        e_rows.append(ei)
        w_rows.append(jnp.sum(jnp.where(hit, scores, 0.0), axis=0, keepdims=True))
        sel = sel | hit
        cand = jnp.where(hit, neg_inf, cand)
    top_s = jnp.concatenate(w_rows, axis=0)
    top_w = top_s / jnp.sum(top_s, axis=0, keepdims=True) * ROUTED_SCALE

    sel_b = sel.astype(BF16)
    r_i = lax.broadcasted_iota(I32, (TM, TM), 0)
    c_i = lax.broadcasted_iota(I32, (TM, TM), 1)
    before = (r_i < c_i).astype(BF16)
    e_r = lax.broadcasted_iota(I32, (N_EXPERTS, N_EXPERTS), 0)
    e_c = lax.broadcasted_iota(I32, (N_EXPERTS, N_EXPERTS), 1)
    earlier = (e_c < e_r).astype(BF16)
    lane_t = lax.broadcasted_iota(I32, cnt_ref.shape, 1)
    cnt = cnt_ref[...]
    pos_parts = []
    for j in range(tm // TM):
        sel_j = sel_b[:, j * TM:(j + 1) * TM]
        rank = jnp.dot(sel_j, before, preferred_element_type=F32)
        count = jnp.sum(sel_j.astype(F32), axis=1, keepdims=True)
        granules = jnp.ceil(count * (1.0 / SEG))
        g_b = jnp.broadcast_to(granules, (N_EXPERTS, 128)).astype(BF16)
        seg_start = jnp.dot(earlier, g_b, preferred_element_type=F32)[:, 0:1] * SEG
        pos_parts.append(seg_start + rank)
        cnt = jnp.where(lane_t == i * (tm // TM) + j, granules * SEG, cnt)
    cnt_ref[...] = cnt
    pos = jnp.concatenate(pos_parts, axis=1)
    p_rows = [jnp.sum(jnp.where(iota_e == e_rows[kk], pos, 0.0), axis=0, keepdims=True)
              for kk in range(TOP_K)]
    lpos_ref[...] = jnp.concatenate(p_rows, axis=0).astype(I32)
    tw_ref[...] = top_w

    gu = jnp.dot(hb, wsgu_ref[...], preferred_element_type=F32)
    act = jax.nn.silu(gu[:, :D_SHARED]) * gu[:, D_SHARED:]
    shared = jnp.dot(act.astype(BF16), wsd_ref[...], preferred_element_type=F32)
    base_ref[...] = ALPHA * h + shared


def _position_matrix(lpos, values, out_ref):
    tm = lpos.shape[1]
    lp = lpos.astype(jnp.int16)
    vals = None if values is None else values.astype(BF16)
    iota = lax.broadcasted_iota(jnp.int16, (MAT_ROWS, tm), 0)
    for c in range(TILE_ROWS // MAT_ROWS):
        acc = jnp.zeros((MAT_ROWS, tm), BF16)
        for kk in range(TOP_K):
            hit = iota == lp[kk:kk + 1, :] - jnp.int16(c * MAT_ROWS)
            acc = jnp.where(hit, jnp.bfloat16(1.0) if vals is None else vals[kk:kk + 1, :], acc)
        out_ref[c * MAT_ROWS:(c + 1) * MAT_ROWS, :] = acc


def _for_each_tile_copy(tabs, tile, make_copy, act):
    big_ref, small_ref, nbig_ref, nsmall_ref = tabs
    for tab_ref, n_ref, cap, rows in ((big_ref, nbig_ref, MAX_BIG, CHUNK), (small_ref, nsmall_ref, MAX_SMALL, SEG)):
        def one(j, tab_ref=tab_ref, cap=cap, rows=rows):
            p = tab_ref[tile * cap + j]
            lrow = pl.multiple_of((p & (2 ** LOCAL_BITS - 1)) * SEG, SEG)
            grow = pl.multiple_of(lax.shift_right_logical(p, LOCAL_BITS) * SEG, SEG)
            act(make_copy(lrow, grow, rows))

        n = n_ref[tile]
        n_groups = lax.shift_right_logical(n, UNROLL.bit_length() - 1)

        def group(q, c, one=one):
            for u in range(UNROLL):
                one(q * UNROLL + u)
            return c

        def single(q, c, one=one, n_groups=n_groups):
            one(n_groups * UNROLL + q)
            return c

        lax.fori_loop(0, n_groups, group, 0)
        lax.fori_loop(0, n - n_groups * UNROLL, single, 0)


def _dispatch_kernel(big_ref, small_ref, nbig_ref, nsmall_ref, fs_ref, fn_ref, lpos_ref, hb_ref, buf_ref,
                     sorted_ref, mat_ref, zero_ref, sem, zsem):
    tabs = (big_ref, small_ref, nbig_ref, nsmall_ref)
    i = pl.program_id(0)
    n_i = pl.num_programs(0)
    slot = lax.rem(i, 2)
    _position_matrix(lpos_ref[...], None, mat_ref)
    s = jnp.dot(mat_ref[...], hb_ref[...], preferred_element_type=F32)
    sorted_ref[slot] = (lax.bitcast_convert_type(s[:, :HALF], U32)
                        | (lax.bitcast_convert_type(s[:, HALF:], U32) >> 16))

    def copies_from(sl):
        def make_copy(lrow, grow, rows):
            return pltpu.make_async_copy(sorted_ref.at[sl, pl.ds(lrow, rows)], buf_ref.at[pl.ds(grow, rows)],
                                         sem.at[sl])
        return make_copy

    _for_each_tile_copy(tabs, i, copies_from(slot), lambda cp: cp.start())

    @pl.when(i > 0)
    def _():
        _for_each_tile_copy(tabs, i - 1, copies_from(1 - slot), lambda cp: cp.wait())

    @pl.when(i == n_i - 1)
    def _():
        _for_each_tile_copy(tabs, i, copies_from(slot), lambda cp: cp.wait())
        zero_ref[...] = jnp.zeros_like(zero_ref)

        def zero_copy(grow, rows):
            return pltpu.make_async_copy(zero_ref.at[pl.ds(0, rows)], buf_ref.at[pl.ds(grow, rows)], zsem)

        def fill(act):
            def gap(e, carry):
                gs, n = fs_ref[e], fn_ref[e]
                n_big = lax.shift_right_logical(n, CHUNK.bit_length() - 1)
                n_small = lax.shift_right_logical(n - n_big * CHUNK, SEG.bit_length() - 1)

                def big(q, c):
                    act(zero_copy(pl.multiple_of(gs + q * CHUNK, SEG), CHUNK))
                    return c

                def small(q, c):
                    act(zero_copy(pl.multiple_of(gs + n_big * CHUNK + q * SEG, SEG), SEG))
                    return c

                lax.fori_loop(0, n_big, big, 0)
                lax.fori_loop(0, n_small, small, 0)
                return carry

            lax.fori_loop(0, fs_ref.shape[0], gap, 0)

        fill(lambda cp: cp.start())
        fill(lambda cp: cp.wait())


def _ffn_kernel(rs_ref, nc_ref, tail_ref, x_hbm, wg_ref, wu_ref, wd_ref, o_hbm, xin, yout, wgu_s, wd_s,
                zero_s, st_ref, sem_in, sem_out, zsem):
    e = pl.program_id(0)
    n_e = pl.num_programs(0)

    @pl.when(e == 0)
    def _():
        st_ref[0] = 0
        st_ref[1] = 0

    wgu_s[:, :D_EXPERT] = wg_ref[0].astype(BF16)
    wgu_s[:, D_EXPERT:] = wu_ref[0].astype(BF16)
    wd_s[...] = wd_ref[0].astype(BF16)

    n = nc_ref[e]
    r0 = rs_ref[e]
    g0 = st_ref[0]
    e_next = jnp.minimum(e + 1, n_e - 1)
    feeds_next = (n > 0) & (e + 1 < n_e) & (nc_ref[e_next] > 0)

    def in_copy(slot, row):
        return pltpu.make_async_copy(x_hbm.at[pl.ds(pl.multiple_of(row, SEG), BM)], xin.at[slot], sem_in.at[slot])

    def out_copy(slot, row):
        return pltpu.make_async_copy(yout.at[slot], o_hbm.at[pl.ds(pl.multiple_of(row, SEG), BM)], sem_out.at[slot])

    @pl.when((n > 0) & (st_ref[1] == 0))
    def _():
        in_copy(lax.rem(g0, 2), r0).start()

    def block(c, carry):
        g = g0 + c
        slot = lax.rem(g, 2)
        row = r0 + c * BM

        @pl.when(c + 1 < n)
        def _():
            in_copy(1 - slot, row + BM).start()

        @pl.when((c + 1 == n) & feeds_next)
        def _():
            in_copy(1 - slot, rs_ref[e_next]).start()

        in_copy(slot, row).wait()

        @pl.when(g >= 2)
        def _():
            out_copy(slot, st_ref[2 + slot]).wait()

        lo, hi = _unpack_halves(xin[slot])
        x = jnp.concatenate([lo.astype(BF16), hi.astype(BF16)], axis=1)
        gu = jnp.dot(x, wgu_s[...], preferred_element_type=F32)
        act = jax.nn.silu(gu[:, :D_EXPERT]) * gu[:, D_EXPERT:]
        y = jnp.dot(act.astype(BF16), wd_s[...], preferred_element_type=F32)
        yout[slot] = _pack_halves(y[:, :HALF], y[:, HALF:])
        out_copy(slot, row).start()
        st_ref[2 + slot] = row
        return carry

    lax.fori_loop(0, n, block, 0)
    st_ref[0] = g0 + n
    st_ref[1] = feeds_next.astype(I32)

    @pl.when(e == n_e - 1)
    def _():
        total = g0 + n
        for back in (1, 2):
            @pl.when(total >= back)
            def _(back=back):
                slot = lax.rem(total - back, 2)
                out_copy(slot, st_ref[2 + slot]).wait()

        zero_s[...] = jnp.zeros_like(zero_s)

        def zero_copy(q):
            return pltpu.make_async_copy(zero_s, o_hbm.at[pl.ds(pl.multiple_of(tail_ref[0] + q * BM, SEG), BM)],
                                         zsem)

        def z_start(q, c):
            zero_copy(q).start()
            return c

        def z_wait(q, c):
            zero_copy(q).wait()
            return c

        lax.fori_loop(0, tail_ref[1], z_start, 0)
        lax.fori_loop(0, tail_ref[1], z_wait, 0)


def _combine_kernel(big_ref, small_ref, nbig_ref, nsmall_ref, lpos_ref, tw_ref, base_ref, g2_ref, b2_ref,
                    obuf_ref, out_ref, sorted_ref, mat_ref, sem):
    tabs = (big_ref, small_ref, nbig_ref, nsmall_ref)
    i = pl.program_id(0)
    n_i = pl.num_programs(0)
    slot = lax.rem(i, 2)

    def copies_into(sl):
        def make_copy(lrow, grow, rows):
            return pltpu.make_async_copy(obuf_ref.at[pl.ds(grow, rows)], sorted_ref.at[sl, pl.ds(lrow, rows)],
                                         sem.at[sl])
        return make_copy

    @pl.when(i == 0)
    def _():
        sorted_ref[...] = jnp.zeros_like(sorted_ref)
        _for_each_tile_copy(tabs, i, copies_into(slot), lambda cp: cp.start())

    @pl.when(i + 1 < n_i)
    def _():
        _for_each_tile_copy(tabs, i + 1, copies_into(1 - slot), lambda cp: cp.start())

    _position_matrix(lpos_ref[...], tw_ref[...], mat_ref)

    _for_each_tile_copy(tabs, i, copies_into(slot), lambda cp: cp.wait())
    lo, hi = _unpack_halves(sorted_ref[slot])
    wb = mat_ref[...]
    routed = jnp.concatenate([lax.dot_general(wb, lo.astype(BF16), TN_DIMS, preferred_element_type=F32),
                              lax.dot_general(wb, hi.astype(BF16), TN_DIMS, preferred_element_type=F32)],
                             axis=1)
    z = base_ref[...] + routed
    mu = jnp.mean(z, axis=-1, keepdims=True)
    zc = z - mu
    var = jnp.mean(zc * zc, axis=-1, keepdims=True)
    out_ref[...] = zc * lax.rsqrt(var + LN_EPS) * g2_ref[...] + b2_ref[...]


def _full(shape):
    nd = len(shape)
    return pl.BlockSpec(shape, lambda *_: (0,) * nd)


def _chunk_table(count, first_row, rows, cap, l_start, g_start):
    end = jnp.cumsum(count, axis=1)
    q = jnp.arange(cap, dtype=I32)
    expert = jnp.minimum(jnp.sum((end[:, None, :] <= q[None, :, None]).astype(I32), axis=2), N_EXPERTS - 1)
    hit = expert[:, :, None] == jnp.arange(N_EXPERTS, dtype=I32)[None, None, :]

    def pick(a):
        return jnp.sum(jnp.where(hit, a[:, None, :], 0), axis=2)

    off = pick(first_row) + (q[None, :] - pick(end - count)) * rows
    entry = ((pick(g_start) + off) // SEG) * 2 ** LOCAL_BITS + (pick(l_start) + off) // SEG
    return entry.reshape(-1).astype(I32), end[:, -1].astype(I32)


def _params(*sem):
    return pltpu.CompilerParams(dimension_semantics=sem, vmem_limit_bytes=VMEM_LIMIT)


def kernel(x, w_in, w_pool, pool_scale, q_norm_g, w_q_up, kv_norm_g, w_kv_up, w_out, ln1_g, ln1_b,
           w_router, router_bias, w_gate, w_up, w_down, w_sh_gate, w_sh_up, w_sh_down, ln2_g, ln2_b):
    B, S, D = x.shape
    assert D == D_MODEL and S % TS == 0 and S % TQ == 0 and (B * S) % TMX == 0 and TMX % TM == 0
    assert TS % HALO == 0 and TILE_ROWS < 2 ** 15
    N = B * S
    H = MLA_HEADS
    hw = H * HEAD_PAD

    wi = w_in[0]
    c_r = POOL_WIDTH + Q_LORA + KV_LORA
    rope = wi[:, c_r:c_r + QK_ROPE]
    half = QK_ROPE // 2
    zc64 = jnp.zeros((D, QK_NOPE), F32)
    zc32 = jnp.zeros((D, HEAD_PAD - QK_DIM), F32)
    rope_a = jnp.concatenate([zc64, rope, zc32], axis=1)
    rope_b = jnp.concatenate([zc64, -rope[:, half:], rope[:, :half], zc32], axis=1)
    win_p = jnp.concatenate([wi[:, :c_r], rope_a, rope_b], axis=1).astype(BF16)

    wq = w_q_up[0].reshape(Q_LORA, H, QK_DIM)
    zq = jnp.zeros((Q_LORA, H, HEAD_PAD - QK_DIM), F32)
    wq_a = jnp.concatenate([wq, zq], axis=2).reshape(Q_LORA, hw)
    wq_b = jnp.concatenate([jnp.zeros((Q_LORA, H, QK_NOPE), F32), -wq[:, :, QK_NOPE + half:],
                            wq[:, :, QK_NOPE:QK_NOPE + half], zq], axis=2).reshape(Q_LORA, hw)
    wq_p = jnp.concatenate([wq_a, wq_b], axis=1).astype(BF16)

    wkv = w_kv_up[0].reshape(KV_LORA, H, QK_NOPE + V_HEAD)
    wk = jnp.concatenate([wkv[:, :, :QK_NOPE], jnp.zeros((KV_LORA, H, HEAD_PAD - QK_NOPE), F32)],
                         axis=2).reshape(KV_LORA, hw)
    wk_p = wk.astype(BF16)
    wvt_p = wkv[:, :, QK_NOPE:].reshape(KV_LORA, H * V_HEAD).T.astype(BF16)

    pos = jnp.arange(S, dtype=F32)
    inv_freq = ROPE_THETA ** (-jnp.arange(0, QK_ROPE, 2, dtype=F32) / QK_ROPE)
    ang = pos[:, None] * inv_freq[None, :]
    cosv, sinv = jnp.cos(ang), jnp.sin(ang)
    cos_t = jnp.concatenate([jnp.ones((S, QK_NOPE), F32), cosv, cosv,
                             jnp.zeros((S, HEAD_PAD - QK_DIM), F32)], axis=1)
    sin_t = jnp.concatenate([jnp.zeros((S, QK_NOPE), F32), sinv, sinv,
                             jnp.zeros((S, HEAD_PAD - QK_DIM), F32)], axis=1)

    n_ts = S // TS
    hb = TS // HALO
    ypool, q, k, vt = pl.pallas_call(
        functools.partial(_proj_kernel, seq_len=S),
        grid=(B, n_ts),
        in_specs=[
            pl.BlockSpec((1, TS, D), lambda b, i: (b, i, 0)),
            pl.BlockSpec((1, HALO, D), lambda b, i: (b, jnp.maximum(i * hb - 1, 0), 0)),
            pl.BlockSpec((1, HALO, D), lambda b, i: (b, jnp.minimum((i + 1) * hb, S // HALO - 1), 0)),
            _full(win_p.shape), _full((len(POOL_WINDOWS), POOL_GROUP, POOL_GROUP)),
            _full((1, POOL_WIDTH)), _full((1, Q_LORA)), _full(wq_p.shape), _full((1, KV_LORA)),
            _full(wk_p.shape), _full(wvt_p.shape),
            pl.BlockSpec((TS, HEAD_PAD), lambda b, i: (i, 0)),
            pl.BlockSpec((TS, HEAD_PAD), lambda b, i: (i, 0)),
        ],
        out_specs=[
            pl.BlockSpec((1, TS, POOL_WIDTH), lambda b, i: (b, i, 0)),
            pl.BlockSpec((1, H, TS, HEAD_PAD), lambda b, i: (b, 0, i, 0)),
            pl.BlockSpec((1, H, TS, HEAD_PAD), lambda b, i: (b, 0, i, 0)),
            pl.BlockSpec((1, H * V_HEAD, TS), lambda b, i: (b, 0, i)),
        ],
        out_shape=[
            jax.ShapeDtypeStruct((B, S, POOL_WIDTH), BF16),
            jax.ShapeDtypeStruct((B, H, S, HEAD_PAD), BF16),
            jax.ShapeDtypeStruct((B, H, S, HEAD_PAD), BF16),
            jax.ShapeDtypeStruct((B, H * V_HEAD, S), BF16),
        ],
        scratch_shapes=[pltpu.VMEM((TS + 2 * HALO, POOL_WIDTH), F32)],
        compiler_params=_params("parallel", "arbitrary"),
        name="proj",
    )(x, x, x, win_p, w_pool[0].astype(BF16), pool_scale, q_norm_g, wq_p, kv_norm_g, wk_p, wvt_p,
      cos_t, sin_t)

    ymla = pl.pallas_call(
        _attn_kernel,
        grid=(B, S // TQ),
        in_specs=[
            pl.BlockSpec((1, H, TQ, HEAD_PAD), lambda b, i: (b, 0, i, 0)),
            pl.BlockSpec((1, H, S, HEAD_PAD), lambda b, i: (b, 0, 0, 0)),
            pl.BlockSpec((1, H * V_HEAD, S), lambda b, i: (b, 0, 0)),
        ],
        out_specs=pl.BlockSpec((1, TQ, H * V_HEAD), lambda b, i: (b, i, 0)),
        out_shape=jax.ShapeDtypeStruct((B, S, H * V_HEAD), BF16),
        compiler_params=_params("parallel", "arbitrary"),
        name="attn",
    )(q, k, vt)

    wr_t = w_router[0].T
    wr_hi = wr_t.astype(BF16)
    wr_lo = (wr_t - wr_hi.astype(F32)).astype(BF16)
    wsgu = jnp.concatenate([w_sh_gate[0], w_sh_up[0]], axis=1).astype(BF16)
    n_tm = N // TM
    row_spec = pl.BlockSpec((TMX, D), lambda i: (i, 0))
    half_spec = pl.BlockSpec((TMX, POOL_WIDTH), lambda i: (i, 0))
    tok_spec = pl.BlockSpec((TOP_K, TMX), lambda i: (0, i))
    base, hb, lpos, top_w, tile_cnt = pl.pallas_call(
        _mix_kernel,
        grid=(N // TMX,),
        in_specs=[
            row_spec, half_spec, half_spec, _full((D, D)), _full((1, D)), _full((1, D)),
            _full((N_EXPERTS, D)), _full((N_EXPERTS, D)), _full((N_EXPERTS, 1)),
            _full(wsgu.shape), _full((D_SHARED, D)),
        ],
        out_specs=[row_spec, row_spec, tok_spec, tok_spec, _full((N_EXPERTS, n_tm))],
        out_shape=[
            jax.ShapeDtypeStruct((N, D), F32),
            jax.ShapeDtypeStruct((N, D), BF16),
            jax.ShapeDtypeStruct((TOP_K, N), I32),
            jax.ShapeDtypeStruct((TOP_K, N), F32),
            jax.ShapeDtypeStruct((N_EXPERTS, n_tm), F32),
        ],
        compiler_params=_params("arbitrary"),
        name="mix",
    )(x.reshape(N, D), ypool.reshape(N, POOL_WIDTH), ymla.reshape(N, H * V_HEAD), w_out[0].astype(BF16),
      ln1_g, ln1_b, wr_hi, wr_lo, router_bias.reshape(N_EXPERTS, 1), wsgu, w_sh_down[0].astype(BF16))

    c8 = tile_cnt.T.astype(I32)
    l_start = jnp.cumsum(c8, axis=1) - c8
    expert_rows = jnp.sum(c8, axis=0)
    padded = (expert_rows + BM - 1) // BM * BM
    pad_end = jnp.cumsum(padded)
    g_start = (pad_end - padded)[None, :] + jnp.cumsum(c8, axis=0) - c8
    n_blocks = -(-(N * TOP_K + n_tm * N_EXPERTS * (SEG - 1) + N_EXPERTS * (BM - 1)) // BM)
    P = n_blocks * BM
    n_big = c8 // CHUNK
    big_tab, big_cnt = _chunk_table(n_big, jnp.zeros_like(c8), CHUNK, MAX_BIG, l_start, g_start)
    small_tab, small_cnt = _chunk_table((c8 - n_big * CHUNK) // SEG, n_big * CHUNK, SEG, MAX_SMALL,
                                        l_start, g_start)
    seg_tables = (big_tab, small_tab, big_cnt, small_cnt)
    fill_start = jnp.concatenate([pad_end - padded + expert_rows, pad_end[-1:]]).astype(I32)
    fill_rows = jnp.concatenate([padded - expert_rows, P - pad_end[-1:]]).astype(I32)

    buf = pl.pallas_call(
        _dispatch_kernel,
        grid_spec=pltpu.PrefetchScalarGridSpec(
            num_scalar_prefetch=6, grid=(n_tm,),
            in_specs=[pl.BlockSpec((TOP_K, TM), lambda i, *_: (0, i)),
                      pl.BlockSpec((TM, D), lambda i, *_: (i, 0))],
            out_specs=pl.BlockSpec(memory_space=pl.ANY),
            scratch_shapes=[pltpu.VMEM((2, TILE_ROWS, HALF), U32), pltpu.VMEM((TILE_ROWS, TM), BF16),
                            pltpu.VMEM((CHUNK, HALF), U32),
                            pltpu.SemaphoreType.DMA((2,)), pltpu.SemaphoreType.DMA]),
        out_shape=jax.ShapeDtypeStruct((P, HALF), U32),
        compiler_params=_params("arbitrary"),
        name="dispatch",
    )(*seg_tables, fill_start, fill_rows, lpos, hb)

    row_start = (pad_end - padded).astype(I32)
    tail = jnp.stack([pad_end[-1], (P - pad_end[-1]) // BM]).astype(I32)
    obuf = pl.pallas_call(
        _ffn_kernel,
        grid_spec=pltpu.PrefetchScalarGridSpec(
            num_scalar_prefetch=3, grid=(N_EXPERTS,),
            in_specs=[pl.BlockSpec(memory_space=pl.ANY),
                      pl.BlockSpec((1, D, D_EXPERT), lambda e, *_: (e, 0, 0)),
                      pl.BlockSpec((1, D, D_EXPERT), lambda e, *_: (e, 0, 0)),
                      pl.BlockSpec((1, D_EXPERT, D), lambda e, *_: (e, 0, 0))],
            out_specs=pl.BlockSpec(memory_space=pl.ANY),
            scratch_shapes=[pltpu.VMEM((2, BM, HALF), U32), pltpu.VMEM((2, BM, HALF), U32),
                            pltpu.VMEM((D, 2 * D_EXPERT), BF16), pltpu.VMEM((D_EXPERT, D), BF16),
                            pltpu.VMEM((BM, HALF), U32), pltpu.SMEM((4,), I32),
                            pltpu.SemaphoreType.DMA((2,)), pltpu.SemaphoreType.DMA((2,)),
                            pltpu.SemaphoreType.DMA]),
        out_shape=jax.ShapeDtypeStruct((P, HALF), U32),
        compiler_params=_params("arbitrary"),
        name="ffn",
    )(row_start, (padded // BM).astype(I32), tail, buf, w_gate[0], w_up[0], w_down[0])

    out = pl.pallas_call(
        _combine_kernel,
        grid_spec=pltpu.PrefetchScalarGridSpec(
            num_scalar_prefetch=4, grid=(n_tm,),
            in_specs=[pl.BlockSpec((TOP_K, TM), lambda i, *_: (0, i)),
                      pl.BlockSpec((TOP_K, TM), lambda i, *_: (0, i)),
                      pl.BlockSpec((TM, D), lambda i, *_: (i, 0)),
                      pl.BlockSpec((1, D), lambda i, *_: (0, 0)), pl.BlockSpec((1, D), lambda i, *_: (0, 0)),
                      pl.BlockSpec(memory_space=pl.ANY)],
            out_specs=pl.BlockSpec((TM, D), lambda i, *_: (i, 0)),
            scratch_shapes=[pltpu.VMEM((2, TILE_ROWS, HALF), U32), pltpu.VMEM((TILE_ROWS, TM), BF16),
                            pltpu.SemaphoreType.DMA((2,))]),
        out_shape=jax.ShapeDtypeStruct((N, D), F32),
        compiler_params=_params("arbitrary"),
        name="combine",
    )(*seg_tables, lpos, top_w, base, ln2_g, ln2_b, obuf)
    return out.reshape(B, S, D)
```

```python
import functools

import jax
import jax.numpy as jnp
from jax import lax
from jax.experimental import pallas as pl
from jax.experimental.pallas import tpu as pltpu

F32 = jnp.float32
BF16 = jnp.bfloat16
I32 = jnp.int32
U32 = jnp.uint32

D_MODEL = 1024
POOL_WINDOWS = (2, 4, 8, 16)
POOL_WIDTH = 512
POOL_GROUP = 128
MLA_HEADS = 8
QK_NOPE = 64
QK_ROPE = 32
QK_DIM = QK_NOPE + QK_ROPE
V_HEAD = 64
Q_LORA = 256
KV_LORA = 128
ROPE_THETA = 10000.0
HEAD_PAD = 128
HALO = 8

N_EXPERTS = 64
TOP_K = 8
N_GROUPS = 8
GROUP_SIZE = N_EXPERTS // N_GROUPS
TOPK_GROUPS = 4
D_EXPERT = 256
D_SHARED = 256
ROUTED_SCALE = 2.5

DEPTH = 1
ALPHA = (2 * DEPTH) ** 0.25
LN_EPS = 1e-5
RMS_EPS = 1e-6
LOG2_E = 1.4426950408889634

VMEM_LIMIT = 56 * 1024 * 1024

TS = 512
TQ = 512
TM = 256
TMX = 1024
BM = 1024
HALF = D_MODEL // 2
SEG = 8
CHUNK = 32
ZERO_ROWS = 256
TILE_ROWS = TM * TOP_K + N_EXPERTS * (SEG - 1)
TILE_ROWS += -TILE_ROWS % SEG
MAX_BIG = TILE_ROWS // CHUNK
MAX_SMALL = N_EXPERTS * (CHUNK // SEG - 1)
LOCAL_BITS = (TILE_ROWS // SEG - 1).bit_length()
UNROLL = 4
MAT_ROWS = 64
assert TILE_ROWS % MAT_ROWS == 0

NT_DIMS = (((1,), (1,)), ((), ()))
TN_DIMS = (((0,), (0,)), ((), ()))


def _pack_halves(lo, hi):
    ulo = lax.bitcast_convert_type(lo.astype(BF16).astype(F32), U32)
    uhi = lax.bitcast_convert_type(hi.astype(BF16).astype(F32), U32)
    return ulo | (uhi >> 16)


def _unpack_halves(p):
    lo = lax.bitcast_convert_type(p & jnp.uint32(0xFFFF0000), F32)
    hi = lax.bitcast_convert_type(p << 16, F32)
    return lo, hi


def _proj_kernel(x_ref, xl_ref, xr_ref, win_ref, wpool_ref, pscale_ref, qg_ref, wq_ref, kvg_ref,
                 wk_ref, wvt_ref, cos_ref, sin_ref, ypool_ref, q_ref, k_ref, vt_ref, uext_ref, *, seq_len):
    i = pl.program_id(1)
    n_i = pl.num_programs(1)
    ts = x_ref.shape[1]

    proj = jnp.dot(x_ref[0].astype(BF16), win_ref[...], preferred_element_type=F32)
    u = proj[:, :POOL_WIDTH]

    w_u = win_ref[:, :POOL_WIDTH]
    ul = jnp.dot(xl_ref[0].astype(BF16), w_u, preferred_element_type=F32)
    ur = jnp.dot(xr_ref[0].astype(BF16), w_u, preferred_element_type=F32)
    uext_ref[0:HALO, :] = jnp.where(i > 0, ul, 0.0)
    uext_ref[HALO:HALO + ts, :] = u
    uext_ref[HALO + ts:HALO + ts + HALO, :] = jnp.where(i < n_i - 1, ur, 0.0)

    pos = lax.broadcasted_iota(I32, (ts, 1), 0) + i * ts
    outs = []
    for g, w in enumerate(POOL_WINDOWS):
        left = w // 2
        right = w - 1 - left
        c0, c1 = g * POOL_GROUP, (g + 1) * POOL_GROUP
        acc = uext_ref[HALO - left:HALO - left + ts, c0:c1]
        for j in range(-left + 1, right + 1):
            acc = acc + uext_ref[HALO + j:HALO + j + ts, c0:c1]
        lo = jnp.maximum(pos - left, 0)
        hi = jnp.minimum(pos + right + 1, seq_len)
        pooled = acc / (hi - lo).astype(F32) - u[:, c0:c1]
        outs.append(jnp.dot(pooled.astype(BF16), wpool_ref[g], preferred_element_type=F32))
    ypool_ref[0] = (jnp.concatenate(outs, axis=1) * pscale_ref[...]).astype(BF16)

    cosf = cos_ref[...]
    sinf = sin_ref[...]

    ql = proj[:, POOL_WIDTH:POOL_WIDTH + Q_LORA]
    qn = ql * lax.rsqrt(jnp.mean(ql * ql, axis=-1, keepdims=True) + RMS_EPS) * qg_ref[...]
    qq = jnp.dot(qn.astype(BF16), wq_ref[...], preferred_element_type=F32)
    scale = QK_DIM ** -0.5 * LOG2_E
    hw = MLA_HEADS * HEAD_PAD
    for h in range(MLA_HEADS):
        qa = qq[:, h * HEAD_PAD:(h + 1) * HEAD_PAD]
        qb = qq[:, hw + h * HEAD_PAD:hw + (h + 1) * HEAD_PAD]
        q_ref[0, h] = ((qa * cosf + qb * sinf) * scale).astype(BF16)

    c_kv = POOL_WIDTH + Q_LORA
    kvl = proj[:, c_kv:c_kv + KV_LORA]
    kvn = kvl * lax.rsqrt(jnp.mean(kvl * kvl, axis=-1, keepdims=True) + RMS_EPS) * kvg_ref[...]
    kvb = kvn.astype(BF16)
    kn = jnp.dot(kvb, wk_ref[...], preferred_element_type=F32)
    c_r = c_kv + KV_LORA
    kr = proj[:, c_r:c_r + HEAD_PAD] * cosf + proj[:, c_r + HEAD_PAD:c_r + 2 * HEAD_PAD] * sinf
    for h in range(MLA_HEADS):
        k_ref[0, h] = (kn[:, h * HEAD_PAD:(h + 1) * HEAD_PAD] + kr).astype(BF16)
    vt_ref[0] = lax.dot_general(wvt_ref[...], kvb, NT_DIMS, preferred_element_type=F32).astype(BF16)


def _attn_kernel(q_ref, k_ref, vt_ref, o_ref):
    s_len = k_ref.shape[2]
    ones = jnp.ones((16, s_len), BF16)

    def scores(h):
        return lax.dot_general(k_ref[0, h], q_ref[0, h], NT_DIMS, preferred_element_type=F32)

    st_next = scores(0)
    outs = []
    for h in range(MLA_HEADS):
        st = st_next
        if h + 1 < MLA_HEADS:
            st_next = scores(h + 1)
        p = jnp.exp2(st - jnp.max(st, axis=0, keepdims=True)).astype(BF16)
        v_aug = jnp.concatenate([vt_ref[0, h * V_HEAD:(h + 1) * V_HEAD, :], ones], axis=0)
        ot = jnp.dot(v_aug, p, preferred_element_type=F32)
        outs.append(ot[:V_HEAD] / ot[V_HEAD:V_HEAD + 1])
    o_ref[0] = jnp.concatenate(outs, axis=0).T.astype(BF16)


def _first_argmax(vals, iota, sentinel):
    m = jnp.max(vals, axis=0, keepdims=True)
    idx = jnp.min(jnp.where(vals == m, iota, sentinel), axis=0, keepdims=True)
    return m, idx


def _mix_kernel(x_ref, yp_ref, ym_ref, wout_ref, g1_ref, b1_ref, wrh_ref, wrl_ref, rb_ref, wsgu_ref,
                wsd_ref, base_ref, hb_ref, lpos_ref, tw_ref, cnt_ref):
    i = pl.program_id(0)
    tm = x_ref.shape[0]

    @pl.when(i == 0)
    def _():
        cnt_ref[...] = jnp.zeros_like(cnt_ref)

    mixed = (jnp.dot(yp_ref[...], wout_ref[:POOL_WIDTH, :], preferred_element_type=F32)
             + jnp.dot(ym_ref[...], wout_ref[POOL_WIDTH:, :], preferred_element_type=F32))
    z = ALPHA * x_ref[...] + mixed
    mu = jnp.mean(z, axis=-1, keepdims=True)
    zc = z - mu
    var = jnp.mean(zc * zc, axis=-1, keepdims=True)
    h = zc * lax.rsqrt(var + LN_EPS) * g1_ref[...] + b1_ref[...]
    hb = h.astype(BF16)

    hb_ref[...] = hb

    h_lo = (h - hb.astype(F32)).astype(BF16)
    logits = (lax.dot_general(wrh_ref[...], hb, NT_DIMS, preferred_element_type=F32)
              + lax.dot_general(wrl_ref[...], hb, NT_DIMS, preferred_element_type=F32)
              + lax.dot_general(wrh_ref[...], h_lo, NT_DIMS, preferred_element_type=F32))
    scores = jax.nn.sigmoid(logits)
    biased = scores + rb_ref[...]

    neg_inf = jnp.float32(-jnp.inf)
    iota_g = lax.broadcasted_iota(I32, (GROUP_SIZE, tm), 0)
    grp_rows = []
    for g in range(N_GROUPS):
        bg = biased[g * GROUP_SIZE:(g + 1) * GROUP_SIZE, :]
        m1, i1 = _first_argmax(bg, iota_g, GROUP_SIZE)
        m2 = jnp.max(jnp.where(iota_g == i1, neg_inf, bg), axis=0, keepdims=True)
        grp_rows.append(m1 + m2)
    grp_score = jnp.concatenate(grp_rows, axis=0)

    iota_n = lax.broadcasted_iota(I32, (N_GROUPS, tm), 0)
    grp_sel = jnp.zeros((N_GROUPS, tm), jnp.bool_)
    for _ in range(TOPK_GROUPS):
        _, gi = _first_argmax(grp_score, iota_n, N_GROUPS)
        hit = iota_n == gi
        grp_sel = grp_sel | hit
        grp_score = jnp.where(hit, neg_inf, grp_score)

    cand = jnp.concatenate(
        [jnp.where(grp_sel[g:g + 1, :], biased[g * GROUP_SIZE:(g + 1) * GROUP_SIZE, :], neg_inf)
         for g in range(N_GROUPS)], axis=0)

    iota_e = lax.broadcasted_iota(I32, (N_EXPERTS, tm), 0)
    sel = jnp.zeros((N_EXPERTS, tm), jnp.bool_)
    e_rows, w_rows = [], []
    for _ in range(TOP_K):
        _, ei = _first_argmax(cand, iota_e, N_EXPERTS)
        hit = iota_e == ei
        e_rows.append(ei)
        w_rows.append(jnp.sum(jnp.where(hit, scores, 0.0), axis=0, keepdims=True))
        sel = sel | hit
        cand = jnp.where(hit, neg_inf, cand)
    top_s = jnp.concatenate(w_rows, axis=0)
    top_w = top_s / jnp.sum(top_s, axis=0, keepdims=True) * ROUTED_SCALE

    sel_b = sel.astype(BF16)
    r_i = lax.broadcasted_iota(I32, (TM, TM), 0)
    c_i = lax.broadcasted_iota(I32, (TM, TM), 1)
    before = (r_i < c_i).astype(BF16)
    e_r = lax.broadcasted_iota(I32, (N_EXPERTS, N_EXPERTS), 0)
    e_c = lax.broadcasted_iota(I32, (N_EXPERTS, N_EXPERTS), 1)
    earlier = (e_c < e_r).astype(BF16)
    lane_t = lax.broadcasted_iota(I32, cnt_ref.shape, 1)
    cnt = cnt_ref[...]
    pos_parts = []
    for j in range(tm // TM):
        sel_j = sel_b[:, j * TM:(j + 1) * TM]
        rank = jnp.dot(sel_j, before, preferred_element_type=F32)
        count = jnp.sum(sel_j.astype(F32), axis=1, keepdims=True)
        granules = jnp.ceil(count * (1.0 / SEG))
        g_b = jnp.broadcast_to(granules, (N_EXPERTS, 128)).astype(BF16)
        seg_start = jnp.dot(earlier, g_b, preferred_element_type=F32)[:, 0:1] * SEG
        pos_parts.append(seg_start + rank)
        cnt = jnp.where(lane_t == i * (tm // TM) + j, granules * SEG, cnt)
    cnt_ref[...] = cnt
    pos = jnp.concatenate(pos_parts, axis=1)
    p_rows = [jnp.sum(jnp.where(iota_e == e_rows[kk], pos, 0.0), axis=0, keepdims=True)
              for kk in range(TOP_K)]
    lpos_ref[...] = jnp.concatenate(p_rows, axis=0).astype(I32)
    tw_ref[...] = top_w

    gu = jnp.dot(hb, wsgu_ref[...], preferred_element_type=F32)
    act = jax.nn.silu(gu[:, :D_SHARED]) * gu[:, D_SHARED:]
    shared = jnp.dot(act.astype(BF16), wsd_ref[...], preferred_element_type=F32)
    base_ref[...] = ALPHA * h + shared


def _position_matrix(lpos, values, out_ref):
    tm = lpos.shape[1]
    lp = lpos.astype(jnp.int16)
    vals = None if values is None else values.astype(BF16)
    iota = lax.broadcasted_iota(jnp.int16, (MAT_ROWS, tm), 0)
    for c in range(TILE_ROWS // MAT_ROWS):
        acc = jnp.zeros((MAT_ROWS, tm), BF16)
        for kk in range(TOP_K):
            hit = iota == lp[kk:kk + 1, :] - jnp.int16(c * MAT_ROWS)
            acc = jnp.where(hit, jnp.bfloat16(1.0) if vals is None else vals[kk:kk + 1, :], acc)
        out_ref[c * MAT_ROWS:(c + 1) * MAT_ROWS, :] = acc


def _for_each_tile_copy(tabs, tile, make_copy, act):
    big_ref, small_ref, nbig_ref, nsmall_ref = tabs
    for tab_ref, n_ref, cap, rows in ((big_ref, nbig_ref, MAX_BIG, CHUNK), (small_ref, nsmall_ref, MAX_SMALL, SEG)):
        def one(j, tab_ref=tab_ref, cap=cap, rows=rows):
            p = tab_ref[tile * cap + j]
            lrow = pl.multiple_of((p & (2 ** LOCAL_BITS - 1)) * SEG, SEG)
            grow = pl.multiple_of(lax.shift_right_logical(p, LOCAL_BITS) * SEG, SEG)
            act(make_copy(lrow, grow, rows))

        n = n_ref[tile]
        n_groups = lax.shift_right_logical(n, UNROLL.bit_length() - 1)

        def group(q, c, one=one):
            for u in range(UNROLL):
                one(q * UNROLL + u)
            return c

        def single(q, c, one=one, n_groups=n_groups):
            one(n_groups * UNROLL + q)
            return c

        lax.fori_loop(0, n_groups, group, 0)
        lax.fori_loop(0, n - n_groups * UNROLL, single, 0)


def _dispatch_kernel(big_ref, small_ref, nbig_ref, nsmall_ref, fs_ref, fn_ref, lpos_ref, hb_ref, buf_ref,
                     sorted_ref, mat_ref, zero_ref, sem, zsem):
    tabs = (big_ref, small_ref, nbig_ref, nsmall_ref)
    i = pl.program_id(0)
    n_i = pl.num_programs(0)
    slot = lax.rem(i, 2)
    _position_matrix(lpos_ref[...], None, mat_ref)
    s = jnp.dot(mat_ref[...], hb_ref[...], preferred_element_type=F32)
    sorted_ref[slot] = (lax.bitcast_convert_type(s[:, :HALF], U32)
                        | (lax.bitcast_convert_type(s[:, HALF:], U32) >> 16))

    def copies_from(sl):
        def make_copy(lrow, grow, rows):
            return pltpu.make_async_copy(sorted_ref.at[sl, pl.ds(lrow, rows)], buf_ref.at[pl.ds(grow, rows)],
                                         sem.at[sl])
        return make_copy

    _for_each_tile_copy(tabs, i, copies_from(slot), lambda cp: cp.start())

    @pl.when(i > 0)
    def _():
        _for_each_tile_copy(tabs, i - 1, copies_from(1 - slot), lambda cp: cp.wait())

    @pl.when(i == n_i - 1)
    def _():
        _for_each_tile_copy(tabs, i, copies_from(slot), lambda cp: cp.wait())
        zero_ref[...] = jnp.zeros_like(zero_ref)

        def zero_copy(grow, rows):
            return pltpu.make_async_copy(zero_ref.at[pl.ds(0, rows)], buf_ref.at[pl.ds(grow, rows)], zsem)

        def fill(act):
            def gap(e, carry):
                row, left = fs_ref[e], fn_ref[e]
                for rows in (ZERO_ROWS, CHUNK, SEG):
                    count = lax.shift_right_logical(left, rows.bit_length() - 1)

                    def body(q, c, row=row, rows=rows):
                        act(zero_copy(pl.multiple_of(row + q * rows, SEG), rows))
                        return c

                    lax.fori_loop(0, count, body, 0)
                    row = row + count * rows
                    left = left - count * rows
                return carry

            lax.fori_loop(0, fs_ref.shape[0], gap, 0)

        fill(lambda cp: cp.start())
        fill(lambda cp: cp.wait())


def _ffn_kernel(be_ref, nu_ref, x_ref, wg_ref, wu_ref, wd_ref, o_ref, wgu_s, wd_s):
    i = pl.program_id(0)
    n_used = nu_ref[0]

    @pl.when((i == 0) | ((i < n_used) & (be_ref[i] != be_ref[jnp.maximum(i - 1, 0)])))
    def _():
        wgu_s[:, :D_EXPERT] = wg_ref[0].astype(BF16)
        wgu_s[:, D_EXPERT:] = wu_ref[0].astype(BF16)
        wd_s[...] = wd_ref[0].astype(BF16)

    @pl.when(i < n_used)
    def _():
        lo, hi = _unpack_halves(x_ref[...])
        x = jnp.concatenate([lo.astype(BF16), hi.astype(BF16)], axis=1)
        gu = jnp.dot(x, wgu_s[...], preferred_element_type=F32)
        act = jax.nn.silu(gu[:, :D_EXPERT]) * gu[:, D_EXPERT:]
        y = jnp.dot(act.astype(BF16), wd_s[...], preferred_element_type=F32)
        o_ref[...] = _pack_halves(y[:, :HALF], y[:, HALF:])

    @pl.when(i >= n_used)
    def _():
        o_ref[...] = jnp.zeros_like(o_ref)


def _combine_kernel(big_ref, small_ref, nbig_ref, nsmall_ref, lpos_ref, tw_ref, base_ref, g2_ref, b2_ref,
                    obuf_ref, out_ref, sorted_ref, mat_ref, sem):
    tabs = (big_ref, small_ref, nbig_ref, nsmall_ref)
    i = pl.program_id(0)
    n_i = pl.num_programs(0)
    slot = lax.rem(i, 2)

    def copies_into(sl):
        def make_copy(lrow, grow, rows):
            return pltpu.make_async_copy(obuf_ref.at[pl.ds(grow, rows)], sorted_ref.at[sl, pl.ds(lrow, rows)],
                                         sem.at[sl])
        return make_copy

    @pl.when(i == 0)
    def _():
        sorted_ref[...] = jnp.zeros_like(sorted_ref)
        _for_each_tile_copy(tabs, i, copies_into(slot), lambda cp: cp.start())

    @pl.when(i + 1 < n_i)
    def _():
        _for_each_tile_copy(tabs, i + 1, copies_into(1 - slot), lambda cp: cp.start())

    _position_matrix(lpos_ref[...], tw_ref[...], mat_ref)

    _for_each_tile_copy(tabs, i, copies_into(slot), lambda cp: cp.wait())
    lo, hi = _unpack_halves(sorted_ref[slot])
    wb = mat_ref[...]
    routed = jnp.concatenate([lax.dot_general(wb, lo.astype(BF16), TN_DIMS, preferred_element_type=F32),
                              lax.dot_general(wb, hi.astype(BF16), TN_DIMS, preferred_element_type=F32)],
                             axis=1)
    z = base_ref[...] + routed
    mu = jnp.mean(z, axis=-1, keepdims=True)
    zc = z - mu
    var = jnp.mean(zc * zc, axis=-1, keepdims=True)
    out_ref[...] = zc * lax.rsqrt(var + LN_EPS) * g2_ref[...] + b2_ref[...]


def _full(shape):
    nd = len(shape)
    return pl.BlockSpec(shape, lambda *_: (0,) * nd)


def _chunk_table(count, first_row, rows, cap, l_start, g_start):
    end = jnp.cumsum(count, axis=1)
    q = jnp.arange(cap, dtype=I32)
    expert = jnp.minimum(jnp.sum((end[:, None, :] <= q[None, :, None]).astype(I32), axis=2), N_EXPERTS - 1)
    hit = expert[:, :, None] == jnp.arange(N_EXPERTS, dtype=I32)[None, None, :]

    def pick(a):
        return jnp.sum(jnp.where(hit, a[:, None, :], 0), axis=2)

    off = pick(first_row) + (q[None, :] - pick(end - count)) * rows
    entry = ((pick(g_start) + off) // SEG) * 2 ** LOCAL_BITS + (pick(l_start) + off) // SEG
    return entry.reshape(-1).astype(I32), end[:, -1].astype(I32)


def _params(*sem):
    return pltpu.CompilerParams(dimension_semantics=sem, vmem_limit_bytes=VMEM_LIMIT)


def kernel(x, w_in, w_pool, pool_scale, q_norm_g, w_q_up, kv_norm_g, w_kv_up, w_out, ln1_g, ln1_b,
           w_router, router_bias, w_gate, w_up, w_down, w_sh_gate, w_sh_up, w_sh_down, ln2_g, ln2_b):
    B, S, D = x.shape
    assert D == D_MODEL and S % TS == 0 and S % TQ == 0 and (B * S) % TMX == 0 and TMX % TM == 0
    assert TS % HALO == 0 and TILE_ROWS < 2 ** 15
    N = B * S
    H = MLA_HEADS
    hw = H * HEAD_PAD

    wi = w_in[0]
    c_r = POOL_WIDTH + Q_LORA + KV_LORA
    rope = wi[:, c_r:c_r + QK_ROPE]
    half = QK_ROPE // 2
    zc64 = jnp.zeros((D, QK_NOPE), F32)
    zc32 = jnp.zeros((D, HEAD_PAD - QK_DIM), F32)
    rope_a = jnp.concatenate([zc64, rope, zc32], axis=1)
    rope_b = jnp.concatenate([zc64, -rope[:, half:], rope[:, :half], zc32], axis=1)
    win_p = jnp.concatenate([wi[:, :c_r], rope_a, rope_b], axis=1).astype(BF16)

    wq = w_q_up[0].reshape(Q_LORA, H, QK_DIM)
    zq = jnp.zeros((Q_LORA, H, HEAD_PAD - QK_DIM), F32)
    wq_a = jnp.concatenate([wq, zq], axis=2).reshape(Q_LORA, hw)
    wq_b = jnp.concatenate([jnp.zeros((Q_LORA, H, QK_NOPE), F32), -wq[:, :, QK_NOPE + half:],
                            wq[:, :, QK_NOPE:QK_NOPE + half], zq], axis=2).reshape(Q_LORA, hw)
    wq_p = jnp.concatenate([wq_a, wq_b], axis=1).astype(BF16)

    wkv = w_kv_up[0].reshape(KV_LORA, H, QK_NOPE + V_HEAD)
    wk = jnp.concatenate([wkv[:, :, :QK_NOPE], jnp.zeros((KV_LORA, H, HEAD_PAD - QK_NOPE), F32)],
                         axis=2).reshape(KV_LORA, hw)
    wk_p = wk.astype(BF16)
    wvt_p = wkv[:, :, QK_NOPE:].reshape(KV_LORA, H * V_HEAD).T.astype(BF16)

    pos = jnp.arange(S, dtype=F32)
    inv_freq = ROPE_THETA ** (-jnp.arange(0, QK_ROPE, 2, dtype=F32) / QK_ROPE)
    ang = pos[:, None] * inv_freq[None, :]
    cosv, sinv = jnp.cos(ang), jnp.sin(ang)
    cos_t = jnp.concatenate([jnp.ones((S, QK_NOPE), F32), cosv, cosv,
                             jnp.zeros((S, HEAD_PAD - QK_DIM), F32)], axis=1)
    sin_t = jnp.concatenate([jnp.zeros((S, QK_NOPE), F32), sinv, sinv,
                             jnp.zeros((S, HEAD_PAD - QK_DIM), F32)], axis=1)

    n_ts = S // TS
    hb = TS // HALO
    ypool, q, k, vt = pl.pallas_call(
        functools.partial(_proj_kernel, seq_len=S),
        grid=(B, n_ts),
        in_specs=[
            pl.BlockSpec((1, TS, D), lambda b, i: (b, i, 0)),
            pl.BlockSpec((1, HALO, D), lambda b, i: (b, jnp.maximum(i * hb - 1, 0), 0)),
            pl.BlockSpec((1, HALO, D), lambda b, i: (b, jnp.minimum((i + 1) * hb, S // HALO - 1), 0)),
            _full(win_p.shape), _full((len(POOL_WINDOWS), POOL_GROUP, POOL_GROUP)),
            _full((1, POOL_WIDTH)), _full((1, Q_LORA)), _full(wq_p.shape), _full((1, KV_LORA)),
            _full(wk_p.shape), _full(wvt_p.shape),
            pl.BlockSpec((TS, HEAD_PAD), lambda b, i: (i, 0)),
            pl.BlockSpec((TS, HEAD_PAD), lambda b, i: (i, 0)),
        ],
        out_specs=[
            pl.BlockSpec((1, TS, POOL_WIDTH), lambda b, i: (b, i, 0)),
            pl.BlockSpec((1, H, TS, HEAD_PAD), lambda b, i: (b, 0, i, 0)),
            pl.BlockSpec((1, H, TS, HEAD_PAD), lambda b, i: (b, 0, i, 0)),
            pl.BlockSpec((1, H * V_HEAD, TS), lambda b, i: (b, 0, i)),
        ],
        out_shape=[
            jax.ShapeDtypeStruct((B, S, POOL_WIDTH), BF16),
            jax.ShapeDtypeStruct((B, H, S, HEAD_PAD), BF16),
            jax.ShapeDtypeStruct((B, H, S, HEAD_PAD), BF16),
            jax.ShapeDtypeStruct((B, H * V_HEAD, S), BF16),
        ],
        scratch_shapes=[pltpu.VMEM((TS + 2 * HALO, POOL_WIDTH), F32)],
        compiler_params=_params("parallel", "arbitrary"),
        name="proj",
    )(x, x, x, win_p, w_pool[0].astype(BF16), pool_scale, q_norm_g, wq_p, kv_norm_g, wk_p, wvt_p,
      cos_t, sin_t)

    ymla = pl.pallas_call(
        _attn_kernel,
        grid=(B, S // TQ),
        in_specs=[
            pl.BlockSpec((1, H, TQ, HEAD_PAD), lambda b, i: (b, 0, i, 0)),
            pl.BlockSpec((1, H, S, HEAD_PAD), lambda b, i: (b, 0, 0, 0)),
            pl.BlockSpec((1, H * V_HEAD, S), lambda b, i: (b, 0, 0)),
        ],
        out_specs=pl.BlockSpec((1, TQ, H * V_HEAD), lambda b, i: (b, i, 0)),
        out_shape=jax.ShapeDtypeStruct((B, S, H * V_HEAD), BF16),
        compiler_params=_params("parallel", "arbitrary"),
        name="attn",
    )(q, k, vt)

    wr_t = w_router[0].T
    wr_hi = wr_t.astype(BF16)
    wr_lo = (wr_t - wr_hi.astype(F32)).astype(BF16)
    wsgu = jnp.concatenate([w_sh_gate[0], w_sh_up[0]], axis=1).astype(BF16)
    n_tm = N // TM
    row_spec = pl.BlockSpec((TMX, D), lambda i: (i, 0))
    half_spec = pl.BlockSpec((TMX, POOL_WIDTH), lambda i: (i, 0))
    tok_spec = pl.BlockSpec((TOP_K, TMX), lambda i: (0, i))
    base, hb, lpos, top_w, tile_cnt = pl.pallas_call(
        _mix_kernel,
        grid=(N // TMX,),
        in_specs=[
            row_spec, half_spec, half_spec, _full((D, D)), _full((1, D)), _full((1, D)),
            _full((N_EXPERTS, D)), _full((N_EXPERTS, D)), _full((N_EXPERTS, 1)),
            _full(wsgu.shape), _full((D_SHARED, D)),
        ],
        out_specs=[row_spec, row_spec, tok_spec, tok_spec, _full((N_EXPERTS, n_tm))],
        out_shape=[
            jax.ShapeDtypeStruct((N, D), F32),
            jax.ShapeDtypeStruct((N, D), BF16),
            jax.ShapeDtypeStruct((TOP_K, N), I32),
            jax.ShapeDtypeStruct((TOP_K, N), F32),
            jax.ShapeDtypeStruct((N_EXPERTS, n_tm), F32),
        ],
        compiler_params=_params("arbitrary"),
        name="mix",
    )(x.reshape(N, D), ypool.reshape(N, POOL_WIDTH), ymla.reshape(N, H * V_HEAD), w_out[0].astype(BF16),
      ln1_g, ln1_b, wr_hi, wr_lo, router_bias.reshape(N_EXPERTS, 1), wsgu, w_sh_down[0].astype(BF16))

    c8 = tile_cnt.T.astype(I32)
    l_start = jnp.cumsum(c8, axis=1) - c8
    expert_rows = jnp.sum(c8, axis=0)
    padded = (expert_rows + BM - 1) // BM * BM
    pad_end = jnp.cumsum(padded)
    g_start = (pad_end - padded)[None, :] + jnp.cumsum(c8, axis=0) - c8
    n_blocks = -(-(N * TOP_K + n_tm * N_EXPERTS * (SEG - 1) + N_EXPERTS * (BM - 1)) // BM)
    P = n_blocks * BM
    blk_row = jnp.arange(n_blocks, dtype=I32) * BM
    block_e = jnp.minimum(jnp.sum((pad_end[None, :] <= blk_row[:, None]).astype(I32), axis=1), N_EXPERTS - 1)
    n_used = (pad_end[-1] // BM).astype(I32).reshape(1)
    n_big = c8 // CHUNK
    big_tab, big_cnt = _chunk_table(n_big, jnp.zeros_like(c8), CHUNK, MAX_BIG, l_start, g_start)
    small_tab, small_cnt = _chunk_table((c8 - n_big * CHUNK) // SEG, n_big * CHUNK, SEG, MAX_SMALL,
                                        l_start, g_start)
    seg_tables = (big_tab, small_tab, big_cnt, small_cnt)
    fill_start = jnp.concatenate([pad_end - padded + expert_rows, pad_end[-1:]]).astype(I32)
    fill_rows = jnp.concatenate([padded - expert_rows, P - pad_end[-1:]]).astype(I32)

    buf = pl.pallas_call(
        _dispatch_kernel,
        grid_spec=pltpu.PrefetchScalarGridSpec(
            num_scalar_prefetch=6, grid=(n_tm,),
            in_specs=[pl.BlockSpec((TOP_K, TM), lambda i, *_: (0, i)),
                      pl.BlockSpec((TM, D), lambda i, *_: (i, 0))],
            out_specs=pl.BlockSpec(memory_space=pl.ANY),
            scratch_shapes=[pltpu.VMEM((2, TILE_ROWS, HALF), U32), pltpu.VMEM((TILE_ROWS, TM), BF16),
                            pltpu.VMEM((ZERO_ROWS, HALF), U32),
                            pltpu.SemaphoreType.DMA((2,)), pltpu.SemaphoreType.DMA]),
        out_shape=jax.ShapeDtypeStruct((P, HALF), U32),
        compiler_params=_params("arbitrary"),
        name="dispatch",
    )(*seg_tables, fill_start, fill_rows, lpos, hb)

    def used(i, nu):
        return jnp.minimum(i, nu[0] - 1)

    obuf = pl.pallas_call(
        _ffn_kernel,
        grid_spec=pltpu.PrefetchScalarGridSpec(
            num_scalar_prefetch=2, grid=(n_blocks,),
            in_specs=[pl.BlockSpec((BM, HALF), lambda i, be, nu: (used(i, nu), 0)),
                      pl.BlockSpec((1, D, D_EXPERT), lambda i, be, nu: (be[used(i, nu)], 0, 0)),
                      pl.BlockSpec((1, D, D_EXPERT), lambda i, be, nu: (be[used(i, nu)], 0, 0)),
                      pl.BlockSpec((1, D_EXPERT, D), lambda i, be, nu: (be[used(i, nu)], 0, 0))],
            out_specs=pl.BlockSpec((BM, HALF), lambda i, be, nu: (i, 0)),
            scratch_shapes=[pltpu.VMEM((D, 2 * D_EXPERT), BF16), pltpu.VMEM((D_EXPERT, D), BF16)]),
        out_shape=jax.ShapeDtypeStruct((P, HALF), U32),
        compiler_params=_params("arbitrary"),
        name="ffn",
    )(block_e, n_used, buf, w_gate[0], w_up[0], w_down[0])

    out = pl.pallas_call(
        _combine_kernel,
        grid_spec=pltpu.PrefetchScalarGridSpec(
            num_scalar_prefetch=4, grid=(n_tm,),
            in_specs=[pl.BlockSpec((TOP_K, TM), lambda i, *_: (0, i)),
                      pl.BlockSpec((TOP_K, TM), lambda i, *_: (0, i)),
                      pl.BlockSpec((TM, D), lambda i, *_: (i, 0)),
                      pl.BlockSpec((1, D), lambda i, *_: (0, 0)), pl.BlockSpec((1, D), lambda i, *_: (0, 0)),
                      pl.BlockSpec(memory_space=pl.ANY)],
            out_specs=pl.BlockSpec((TM, D), lambda i, *_: (i, 0)),
            scratch_shapes=[pltpu.VMEM((2, TILE_ROWS, HALF), U32), pltpu.VMEM((TILE_ROWS, TM), BF16),
                            pltpu.SemaphoreType.DMA((2,))]),
        out_shape=jax.ShapeDtypeStruct((N, D), F32),
        compiler_params=_params("arbitrary"),
        name="combine",
    )(*seg_tables, lpos, top_w, base, ln2_g, ln2_b, obuf)
    return out.reshape(B, S, D)
```

```python
import functools

import jax
import jax.numpy as jnp
from jax import lax
from jax.experimental import pallas as pl
from jax.experimental.pallas import tpu as pltpu

F32 = jnp.float32
BF16 = jnp.bfloat16
I32 = jnp.int32
U32 = jnp.uint32

D_MODEL = 1024
POOL_WINDOWS = (2, 4, 8, 16)
POOL_WIDTH = 512
POOL_GROUP = 128
MLA_HEADS = 8
QK_NOPE = 64
QK_ROPE = 32
QK_DIM = QK_NOPE + QK_ROPE
V_HEAD = 64
Q_LORA = 256
KV_LORA = 128
ROPE_THETA = 10000.0
HEAD_PAD = 128
HALO = 8

N_EXPERTS = 64
TOP_K = 8
N_GROUPS = 8
GROUP_SIZE = N_EXPERTS // N_GROUPS
TOPK_GROUPS = 4
D_EXPERT = 256
D_SHARED = 256
ROUTED_SCALE = 2.5

DEPTH = 1
ALPHA = (2 * DEPTH) ** 0.25
LN_EPS = 1e-5
RMS_EPS = 1e-6
LOG2_E = 1.4426950408889634

VMEM_LIMIT = 56 * 1024 * 1024

TS = 1024
TQ = 512
TM = 256
TMX = 1024
BM = 1024
HALF = D_MODEL // 2
SEG = 8
CHUNK = 32
ZERO_ROWS = 256
TILE_ROWS = TM * TOP_K + N_EXPERTS * (SEG - 1)
TILE_ROWS += -TILE_ROWS % SEG
MAX_BIG = TILE_ROWS // CHUNK
MAX_SMALL = N_EXPERTS * (CHUNK // SEG - 1)
LOCAL_BITS = (TILE_ROWS // SEG - 1).bit_length()
UNROLL = 4
MAT_ROWS = 64
assert TILE_ROWS % MAT_ROWS == 0

NT_DIMS = (((1,), (1,)), ((), ()))
TN_DIMS = (((0,), (0,)), ((), ()))


def _pack_halves(lo, hi):
    ulo = lax.bitcast_convert_type(lo.astype(BF16).astype(F32), U32)
    uhi = lax.bitcast_convert_type(hi.astype(BF16).astype(F32), U32)
    return ulo | (uhi >> 16)


def _unpack_halves(p):
    lo = lax.bitcast_convert_type(p & jnp.uint32(0xFFFF0000), F32)
    hi = lax.bitcast_convert_type(p << 16, F32)
    return lo, hi


def _proj_kernel(x_ref, xl_ref, xr_ref, win_ref, wpool_ref, pscale_ref, qg_ref, wq_ref, kvg_ref,
                 wk_ref, wvt_ref, cos_ref, sin_ref, ypool_ref, q_ref, k_ref, vt_ref, uext_ref, *, seq_len):
    i = pl.program_id(1)
    n_i = pl.num_programs(1)
    ts = x_ref.shape[1]

    proj = jnp.dot(x_ref[0].astype(BF16), win_ref[...], preferred_element_type=F32)
    u = proj[:, :POOL_WIDTH]

    w_u = win_ref[:, :POOL_WIDTH]
    ul = jnp.dot(xl_ref[0].astype(BF16), w_u, preferred_element_type=F32)
    ur = jnp.dot(xr_ref[0].astype(BF16), w_u, preferred_element_type=F32)
    uext_ref[0:HALO, :] = jnp.where(i > 0, ul, 0.0)
    uext_ref[HALO:HALO + ts, :] = u
    uext_ref[HALO + ts:HALO + ts + HALO, :] = jnp.where(i < n_i - 1, ur, 0.0)

    pos = lax.broadcasted_iota(I32, (ts, 1), 0) + i * ts
    outs = []
    for g, w in enumerate(POOL_WINDOWS):
        left = w // 2
        right = w - 1 - left
        c0, c1 = g * POOL_GROUP, (g + 1) * POOL_GROUP
        acc = uext_ref[HALO - left:HALO - left + ts, c0:c1]
        for j in range(-left + 1, right + 1):
            acc = acc + uext_ref[HALO + j:HALO + j + ts, c0:c1]
        lo = jnp.maximum(pos - left, 0)
        hi = jnp.minimum(pos + right + 1, seq_len)
        pooled = acc / (hi - lo).astype(F32) - u[:, c0:c1]
        outs.append(jnp.dot(pooled.astype(BF16), wpool_ref[g], preferred_element_type=F32))
    ypool_ref[0] = (jnp.concatenate(outs, axis=1) * pscale_ref[...]).astype(BF16)

    cosf = cos_ref[...]
    sinf = sin_ref[...]

    ql = proj[:, POOL_WIDTH:POOL_WIDTH + Q_LORA]
    qn = ql * lax.rsqrt(jnp.mean(ql * ql, axis=-1, keepdims=True) + RMS_EPS) * qg_ref[...]
    qq = jnp.dot(qn.astype(BF16), wq_ref[...], preferred_element_type=F32)
    scale = QK_DIM ** -0.5 * LOG2_E
    hw = MLA_HEADS * HEAD_PAD
    for h in range(MLA_HEADS):
        qa = qq[:, h * HEAD_PAD:(h + 1) * HEAD_PAD]
        qb = qq[:, hw + h * HEAD_PAD:hw + (h + 1) * HEAD_PAD]
        q_ref[0, h] = ((qa * cosf + qb * sinf) * scale).astype(BF16)

    c_kv = POOL_WIDTH + Q_LORA
    kvl = proj[:, c_kv:c_kv + KV_LORA]
    kvn = kvl * lax.rsqrt(jnp.mean(kvl * kvl, axis=-1, keepdims=True) + RMS_EPS) * kvg_ref[...]
    kvb = kvn.astype(BF16)
    kn = jnp.dot(kvb, wk_ref[...], preferred_element_type=F32)
    c_r = c_kv + KV_LORA
    kr = proj[:, c_r:c_r + HEAD_PAD] * cosf + proj[:, c_r + HEAD_PAD:c_r + 2 * HEAD_PAD] * sinf
    for h in range(MLA_HEADS):
        k_ref[0, h] = (kn[:, h * HEAD_PAD:(h + 1) * HEAD_PAD] + kr).astype(BF16)
    vt_ref[0] = lax.dot_general(wvt_ref[...], kvb, NT_DIMS, preferred_element_type=F32).astype(BF16)


def _attn_kernel(q_ref, k_ref, vt_ref, o_ref):
    s_len = k_ref.shape[2]
    ones = jnp.ones((16, s_len), BF16)

    def scores(h):
        return lax.dot_general(k_ref[0, h], q_ref[0, h], NT_DIMS, preferred_element_type=F32)

    st_next = scores(0)
    outs = []
    for h in range(MLA_HEADS):
        st = st_next
        if h + 1 < MLA_HEADS:
            st_next = scores(h + 1)
        p = jnp.exp2(st - jnp.max(st, axis=0, keepdims=True)).astype(BF16)
        v_aug = jnp.concatenate([vt_ref[0, h * V_HEAD:(h + 1) * V_HEAD, :], ones], axis=0)
        ot = jnp.dot(v_aug, p, preferred_element_type=F32)
        outs.append(ot[:V_HEAD] / ot[V_HEAD:V_HEAD + 1])
    o_ref[0] = jnp.concatenate(outs, axis=0).T.astype(BF16)


def _first_argmax(vals, iota, sentinel):
    m = jnp.max(vals, axis=0, keepdims=True)
    idx = jnp.min(jnp.where(vals == m, iota, sentinel), axis=0, keepdims=True)
    return m, idx


def _mix_kernel(x_ref, yp_ref, ym_ref, wout_ref, g1_ref, b1_ref, wrh_ref, wrl_ref, rb_ref, wsgu_ref,
                wsd_ref, base_ref, hb_ref, lpos_ref, tw_ref, cnt_ref):
    i = pl.program_id(0)
    tm = x_ref.shape[0]

    @pl.when(i == 0)
    def _():
        cnt_ref[...] = jnp.zeros_like(cnt_ref)

    mixed = (jnp.dot(yp_ref[...], wout_ref[:POOL_WIDTH, :], preferred_element_type=F32)
             + jnp.dot(ym_ref[...], wout_ref[POOL_WIDTH:, :], preferred_element_type=F32))
    z = ALPHA * x_ref[...] + mixed
    mu = jnp.mean(z, axis=-1, keepdims=True)
    zc = z - mu
    var = jnp.mean(zc * zc, axis=-1, keepdims=True)
    h = zc * lax.rsqrt(var + LN_EPS) * g1_ref[...] + b1_ref[...]
    hb = h.astype(BF16)

    hb_ref[...] = hb

    h_lo = (h - hb.astype(F32)).astype(BF16)
    logits = (lax.dot_general(wrh_ref[...], hb, NT_DIMS, preferred_element_type=F32)
              + lax.dot_general(wrl_ref[...], hb, NT_DIMS, preferred_element_type=F32)
              + lax.dot_general(wrh_ref[...], h_lo, NT_DIMS, preferred_element_type=F32))
    scores = jax.nn.sigmoid(logits)
    biased = scores + rb_ref[...]

    neg_inf = jnp.float32(-jnp.inf)
    iota_g = lax.broadcasted_iota(I32, (GROUP_SIZE, tm), 0)
    grp_rows = []
    for g in range(N_GROUPS):
        bg = biased[g * GROUP_SIZE:(g + 1) * GROUP_SIZE, :]
        m1, i1 = _first_argmax(bg, iota_g, GROUP_SIZE)
        m2 = jnp.max(jnp.where(iota_g == i1, neg_inf, bg), axis=0, keepdims=True)
        grp_rows.append(m1 + m2)
    grp_score = jnp.concatenate(grp_rows, axis=0)

    iota_n = lax.broadcasted_iota(I32, (N_GROUPS, tm), 0)
    grp_sel = jnp.zeros((N_GROUPS, tm), jnp.bool_)
    for _ in range(TOPK_GROUPS):
        _, gi = _first_argmax(grp_score, iota_n, N_GROUPS)
        hit = iota_n == gi
        grp_sel = grp_sel | hit
        grp_score = jnp.where(hit, neg_inf, grp_score)

    cand = jnp.concatenate(
        [jnp.where(grp_sel[g:g + 1, :], biased[g * GROUP_SIZE:(g + 1) * GROUP_SIZE, :], neg_inf)
         for g in range(N_GROUPS)], axis=0)

    iota_e = lax.broadcasted_iota(I32, (N_EXPERTS, tm), 0)
    sel = jnp.zeros((N_EXPERTS, tm), jnp.bool_)
    e_rows, w_rows = [], []
    for _ in range(TOP_K):
        _, ei = _first_argmax(cand, iota_e, N_EXPERTS)
        hit = iota_e == ei
        e_rows.append(ei)
        w_rows.append(jnp.sum(jnp.where(hit, scores, 0.0), axis=0, keepdims=True))
        sel = sel | hit
        cand = jnp.where(hit, neg_inf, cand)
    top_s = jnp.concatenate(w_rows, axis=0)
    top_w = top_s / jnp.sum(top_s, axis=0, keepdims=True) * ROUTED_SCALE

    sel_b = sel.astype(BF16)
    r_i = lax.broadcasted_iota(I32, (TM, TM), 0)
    c_i = lax.broadcasted_iota(I32, (TM, TM), 1)
    before = (r_i < c_i).astype(BF16)
    e_r = lax.broadcasted_iota(I32, (N_EXPERTS, N_EXPERTS), 0)
    e_c = lax.broadcasted_iota(I32, (N_EXPERTS, N_EXPERTS), 1)
    earlier = (e_c < e_r).astype(BF16)
    lane_t = lax.broadcasted_iota(I32, cnt_ref.shape, 1)
    cnt = cnt_ref[...]
    pos_parts = []
    for j in range(tm // TM):
        sel_j = sel_b[:, j * TM:(j + 1) * TM]
        rank = jnp.dot(sel_j, before, preferred_element_type=F32)
        count = jnp.sum(sel_j.astype(F32), axis=1, keepdims=True)
        granules = jnp.ceil(count * (1.0 / SEG))
        g_b = jnp.broadcast_to(granules, (N_EXPERTS, 128)).astype(BF16)
        seg_start = jnp.dot(earlier, g_b, preferred_element_type=F32)[:, 0:1] * SEG
        pos_parts.append(seg_start + rank)
        cnt = jnp.where(lane_t == i * (tm // TM) + j, granules * SEG, cnt)
    cnt_ref[...] = cnt
    pos = jnp.concatenate(pos_parts, axis=1)
    p_rows = [jnp.sum(jnp.where(iota_e == e_rows[kk], pos, 0.0), axis=0, keepdims=True)
              for kk in range(TOP_K)]
    lpos_ref[...] = jnp.concatenate(p_rows, axis=0).astype(I32)
    tw_ref[...] = top_w

    gu = jnp.dot(hb, wsgu_ref[...], preferred_element_type=F32)
    act = jax.nn.silu(gu[:, :D_SHARED]) * gu[:, D_SHARED:]
    shared = jnp.dot(act.astype(BF16), wsd_ref[...], preferred_element_type=F32)
    base_ref[...] = ALPHA * h + shared


def _position_matrix(lpos, values, out_ref):
    tm = lpos.shape[1]
    lp = lpos.astype(jnp.int16)
    vals = None if values is None else values.astype(BF16)
    iota = lax.broadcasted_iota(jnp.int16, (MAT_ROWS, tm), 0)
    for c in range(TILE_ROWS // MAT_ROWS):
        acc = jnp.zeros((MAT_ROWS, tm), BF16)
        for kk in range(TOP_K):
            hit = iota == lp[kk:kk + 1, :] - jnp.int16(c * MAT_ROWS)
            acc = jnp.where(hit, jnp.bfloat16(1.0) if vals is None else vals[kk:kk + 1, :], acc)
        out_ref[c * MAT_ROWS:(c + 1) * MAT_ROWS, :] = acc


def _for_each_tile_copy(tabs, tile, make_copy, act):
    big_ref, small_ref, nbig_ref, nsmall_ref = tabs
    for tab_ref, n_ref, cap, rows in ((big_ref, nbig_ref, MAX_BIG, CHUNK), (small_ref, nsmall_ref, MAX_SMALL, SEG)):
        def one(j, tab_ref=tab_ref, cap=cap, rows=rows):
            p = tab_ref[tile * cap + j]
            lrow = pl.multiple_of((p & (2 ** LOCAL_BITS - 1)) * SEG, SEG)
            grow = pl.multiple_of(lax.shift_right_logical(p, LOCAL_BITS) * SEG, SEG)
            act(make_copy(lrow, grow, rows))

        n = n_ref[tile]
        n_groups = lax.shift_right_logical(n, UNROLL.bit_length() - 1)

        def group(q, c, one=one):
            for u in range(UNROLL):
                one(q * UNROLL + u)
            return c

        def single(q, c, one=one, n_groups=n_groups):
            one(n_groups * UNROLL + q)
            return c

        lax.fori_loop(0, n_groups, group, 0)
        lax.fori_loop(0, n - n_groups * UNROLL, single, 0)


def _dispatch_kernel(big_ref, small_ref, nbig_ref, nsmall_ref, fs_ref, fn_ref, lpos_ref, hb_ref, buf_ref,
                     sorted_ref, mat_ref, zero_ref, sem, zsem):
    tabs = (big_ref, small_ref, nbig_ref, nsmall_ref)
    i = pl.program_id(0)
    n_i = pl.num_programs(0)
    slot = lax.rem(i, 2)
    _position_matrix(lpos_ref[...], None, mat_ref)
    s = jnp.dot(mat_ref[...], hb_ref[...], preferred_element_type=F32)
    sorted_ref[slot] = (lax.bitcast_convert_type(s[:, :HALF], U32)
                        | (lax.bitcast_convert_type(s[:, HALF:], U32) >> 16))

    def copies_from(sl):
        def make_copy(lrow, grow, rows):
            return pltpu.make_async_copy(sorted_ref.at[sl, pl.ds(lrow, rows)], buf_ref.at[pl.ds(grow, rows)],
                                         sem.at[sl])
        return make_copy

    _for_each_tile_copy(tabs, i, copies_from(slot), lambda cp: cp.start())

    @pl.when(i > 0)
    def _():
        _for_each_tile_copy(tabs, i - 1, copies_from(1 - slot), lambda cp: cp.wait())

    @pl.when(i == n_i - 1)
    def _():
        _for_each_tile_copy(tabs, i, copies_from(slot), lambda cp: cp.wait())
        zero_ref[...] = jnp.zeros_like(zero_ref)

        def zero_copy(grow, rows):
            return pltpu.make_async_copy(zero_ref.at[pl.ds(0, rows)], buf_ref.at[pl.ds(grow, rows)], zsem)

        def fill(act):
            def gap(e, carry):
                row, left = fs_ref[e], fn_ref[e]
                for rows in (ZERO_ROWS, CHUNK, SEG):
                    count = lax.shift_right_logical(left, rows.bit_length() - 1)

                    def body(q, c, row=row, rows=rows):
                        act(zero_copy(pl.multiple_of(row + q * rows, SEG), rows))
                        return c

                    lax.fori_loop(0, count, body, 0)
                    row = row + count * rows
                    left = left - count * rows
                return carry

            lax.fori_loop(0, fs_ref.shape[0], gap, 0)

        fill(lambda cp: cp.start())
        fill(lambda cp: cp.wait())


def _ffn_kernel(be_ref, nu_ref, x_ref, wg_ref, wu_ref, wd_ref, o_ref, wgu_s, wd_s):
    i = pl.program_id(0)
    n_used = nu_ref[0]

    @pl.when((i == 0) | ((i < n_used) & (be_ref[i] != be_ref[jnp.maximum(i - 1, 0)])))
    def _():
        wgu_s[:, :D_EXPERT] = wg_ref[0].astype(BF16)
        wgu_s[:, D_EXPERT:] = wu_ref[0].astype(BF16)
        wd_s[...] = wd_ref[0].astype(BF16)

    @pl.when(i < n_used)
    def _():
        lo, hi = _unpack_halves(x_ref[...])
        x = jnp.concatenate([lo.astype(BF16), hi.astype(BF16)], axis=1)
        gu = jnp.dot(x, wgu_s[...], preferred_element_type=F32)
        act = jax.nn.silu(gu[:, :D_EXPERT]) * gu[:, D_EXPERT:]
        y = jnp.dot(act.astype(BF16), wd_s[...], preferred_element_type=F32)
        o_ref[...] = _pack_halves(y[:, :HALF], y[:, HALF:])

    @pl.when(i >= n_used)
    def _():
        o_ref[...] = jnp.zeros_like(o_ref)


def _combine_kernel(big_ref, small_ref, nbig_ref, nsmall_ref, lpos_ref, tw_ref, base_ref, g2_ref, b2_ref,
                    obuf_ref, out_ref, sorted_ref, mat_ref, sem):
    tabs = (big_ref, small_ref, nbig_ref, nsmall_ref)
    i = pl.program_id(0)
    n_i = pl.num_programs(0)
    slot = lax.rem(i, 2)

    def copies_into(sl):
        def make_copy(lrow, grow, rows):
            return pltpu.make_async_copy(obuf_ref.at[pl.ds(grow, rows)], sorted_ref.at[sl, pl.ds(lrow, rows)],
                                         sem.at[sl])
        return make_copy

    @pl.when(i == 0)
    def _():
        sorted_ref[...] = jnp.zeros_like(sorted_ref)
        _for_each_tile_copy(tabs, i, copies_into(slot), lambda cp: cp.start())

    @pl.when(i + 1 < n_i)
    def _():
        _for_each_tile_copy(tabs, i + 1, copies_into(1 - slot), lambda cp: cp.start())

    _position_matrix(lpos_ref[...], tw_ref[...], mat_ref)

    _for_each_tile_copy(tabs, i, copies_into(slot), lambda cp: cp.wait())
    lo, hi = _unpack_halves(sorted_ref[slot])
    wb = mat_ref[...]
    routed = jnp.concatenate([lax.dot_general(wb, lo.astype(BF16), TN_DIMS, preferred_element_type=F32),
                              lax.dot_general(wb, hi.astype(BF16), TN_DIMS, preferred_element_type=F32)],
                             axis=1)
    z = base_ref[...] + routed
    mu = jnp.mean(z, axis=-1, keepdims=True)
    zc = z - mu
    var = jnp.mean(zc * zc, axis=-1, keepdims=True)
    out_ref[...] = zc * lax.rsqrt(var + LN_EPS) * g2_ref[...] + b2_ref[...]


def _full(shape):
    nd = len(shape)
    return pl.BlockSpec(shape, lambda *_: (0,) * nd)


def _chunk_table(count, first_row, rows, cap, l_start, g_start):
    end = jnp.cumsum(count, axis=1)
    q = jnp.arange(cap, dtype=I32)
    expert = jnp.minimum(jnp.sum((end[:, None, :] <= q[None, :, None]).astype(I32), axis=2), N_EXPERTS - 1)
    hit = expert[:, :, None] == jnp.arange(N_EXPERTS, dtype=I32)[None, None, :]

    def pick(a):
        return jnp.sum(jnp.where(hit, a[:, None, :], 0), axis=2)

    off = pick(first_row) + (q[None, :] - pick(end - count)) * rows
    entry = ((pick(g_start) + off) // SEG) * 2 ** LOCAL_BITS + (pick(l_start) + off) // SEG
    return entry.reshape(-1).astype(I32), end[:, -1].astype(I32)


def _params(*sem):
    return pltpu.CompilerParams(dimension_semantics=sem, vmem_limit_bytes=VMEM_LIMIT)


def kernel(x, w_in, w_pool, pool_scale, q_norm_g, w_q_up, kv_norm_g, w_kv_up, w_out, ln1_g, ln1_b,
           w_router, router_bias, w_gate, w_up, w_down, w_sh_gate, w_sh_up, w_sh_down, ln2_g, ln2_b):
    B, S, D = x.shape
    assert D == D_MODEL and S % TS == 0 and S % TQ == 0 and (B * S) % TMX == 0 and TMX % TM == 0
    assert TS % HALO == 0 and TILE_ROWS < 2 ** 15
    N = B * S
    H = MLA_HEADS
    hw = H * HEAD_PAD

    wi = w_in[0]
    c_r = POOL_WIDTH + Q_LORA + KV_LORA
    rope = wi[:, c_r:c_r + QK_ROPE]
    half = QK_ROPE // 2
    zc64 = jnp.zeros((D, QK_NOPE), F32)
    zc32 = jnp.zeros((D, HEAD_PAD - QK_DIM), F32)
    rope_a = jnp.concatenate([zc64, rope, zc32], axis=1)
    rope_b = jnp.concatenate([zc64, -rope[:, half:], rope[:, :half], zc32], axis=1)
    win_p = jnp.concatenate([wi[:, :c_r], rope_a, rope_b], axis=1).astype(BF16)

    wq = w_q_up[0].reshape(Q_LORA, H, QK_DIM)
    zq = jnp.zeros((Q_LORA, H, HEAD_PAD - QK_DIM), F32)
    wq_a = jnp.concatenate([wq, zq], axis=2).reshape(Q_LORA, hw)
    wq_b = jnp.concatenate([jnp.zeros((Q_LORA, H, QK_NOPE), F32), -wq[:, :, QK_NOPE + half:],
                            wq[:, :, QK_NOPE:QK_NOPE + half], zq], axis=2).reshape(Q_LORA, hw)
    wq_p = jnp.concatenate([wq_a, wq_b], axis=1).astype(BF16)

    wkv = w_kv_up[0].reshape(KV_LORA, H, QK_NOPE + V_HEAD)
    wk = jnp.concatenate([wkv[:, :, :QK_NOPE], jnp.zeros((KV_LORA, H, HEAD_PAD - QK_NOPE), F32)],
                         axis=2).reshape(KV_LORA, hw)
    wk_p = wk.astype(BF16)
    wvt_p = wkv[:, :, QK_NOPE:].reshape(KV_LORA, H * V_HEAD).T.astype(BF16)

    pos = jnp.arange(S, dtype=F32)
    inv_freq = ROPE_THETA ** (-jnp.arange(0, QK_ROPE, 2, dtype=F32) / QK_ROPE)
    ang = pos[:, None] * inv_freq[None, :]
    cosv, sinv = jnp.cos(ang), jnp.sin(ang)
    cos_t = jnp.concatenate([jnp.ones((S, QK_NOPE), F32), cosv, cosv,
                             jnp.zeros((S, HEAD_PAD - QK_DIM), F32)], axis=1)
    sin_t = jnp.concatenate([jnp.zeros((S, QK_NOPE), F32), sinv, sinv,
                             jnp.zeros((S, HEAD_PAD - QK_DIM), F32)], axis=1)

    n_ts = S // TS
    hb = TS // HALO
    ypool, q, k, vt = pl.pallas_call(
        functools.partial(_proj_kernel, seq_len=S),
        grid=(B, n_ts),
        in_specs=[
            pl.BlockSpec((1, TS, D), lambda b, i: (b, i, 0)),
            pl.BlockSpec((1, HALO, D), lambda b, i: (b, jnp.maximum(i * hb - 1, 0), 0)),
            pl.BlockSpec((1, HALO, D), lambda b, i: (b, jnp.minimum((i + 1) * hb, S // HALO - 1), 0)),
            _full(win_p.shape), _full((len(POOL_WINDOWS), POOL_GROUP, POOL_GROUP)),
            _full((1, POOL_WIDTH)), _full((1, Q_LORA)), _full(wq_p.shape), _full((1, KV_LORA)),
            _full(wk_p.shape), _full(wvt_p.shape),
            pl.BlockSpec((TS, HEAD_PAD), lambda b, i: (i, 0)),
            pl.BlockSpec((TS, HEAD_PAD), lambda b, i: (i, 0)),
        ],
        out_specs=[
            pl.BlockSpec((1, TS, POOL_WIDTH), lambda b, i: (b, i, 0)),
            pl.BlockSpec((1, H, TS, HEAD_PAD), lambda b, i: (b, 0, i, 0)),
            pl.BlockSpec((1, H, TS, HEAD_PAD), lambda b, i: (b, 0, i, 0)),
            pl.BlockSpec((1, H * V_HEAD, TS), lambda b, i: (b, 0, i)),
        ],
        out_shape=[
            jax.ShapeDtypeStruct((B, S, POOL_WIDTH), BF16),
            jax.ShapeDtypeStruct((B, H, S, HEAD_PAD), BF16),
            jax.ShapeDtypeStruct((B, H, S, HEAD_PAD), BF16),
            jax.ShapeDtypeStruct((B, H * V_HEAD, S), BF16),
        ],
        scratch_shapes=[pltpu.VMEM((TS + 2 * HALO, POOL_WIDTH), F32)],
        compiler_params=_params("parallel", "arbitrary"),
        name="proj",
    )(x, x, x, win_p, w_pool[0].astype(BF16), pool_scale, q_norm_g, wq_p, kv_norm_g, wk_p, wvt_p,
      cos_t, sin_t)

    ymla = pl.pallas_call(
        _attn_kernel,
        grid=(B, S // TQ),
        in_specs=[
            pl.BlockSpec((1, H, TQ, HEAD_PAD), lambda b, i: (b, 0, i, 0)),
            pl.BlockSpec((1, H, S, HEAD_PAD), lambda b, i: (b, 0, 0, 0)),
            pl.BlockSpec((1, H * V_HEAD, S), lambda b, i: (b, 0, 0)),
        ],
        out_specs=pl.BlockSpec((1, TQ, H * V_HEAD), lambda b, i: (b, i, 0)),
        out_shape=jax.ShapeDtypeStruct((B, S, H * V_HEAD), BF16),
        compiler_params=_params("parallel", "arbitrary"),
        name="attn",
    )(q, k, vt)

    wr_t = w_router[0].T
    wr_hi = wr_t.astype(BF16)
    wr_lo = (wr_t - wr_hi.astype(F32)).astype(BF16)
    wsgu = jnp.concatenate([w_sh_gate[0], w_sh_up[0]], axis=1).astype(BF16)
    n_tm = N // TM
    row_spec = pl.BlockSpec((TMX, D), lambda i: (i, 0))
    half_spec = pl.BlockSpec((TMX, POOL_WIDTH), lambda i: (i, 0))
    tok_spec = pl.BlockSpec((TOP_K, TMX), lambda i: (0, i))
    base, hb, lpos, top_w, tile_cnt = pl.pallas_call(
        _mix_kernel,
        grid=(N // TMX,),
        in_specs=[
            row_spec, half_spec, half_spec, _full((D, D)), _full((1, D)), _full((1, D)),
            _full((N_EXPERTS, D)), _full((N_EXPERTS, D)), _full((N_EXPERTS, 1)),
            _full(wsgu.shape), _full((D_SHARED, D)),
        ],
        out_specs=[row_spec, row_spec, tok_spec, tok_spec, _full((N_EXPERTS, n_tm))],
        out_shape=[
            jax.ShapeDtypeStruct((N, D), F32),
            jax.ShapeDtypeStruct((N, D), BF16),
            jax.ShapeDtypeStruct((TOP_K, N), I32),
            jax.ShapeDtypeStruct((TOP_K, N), F32),
            jax.ShapeDtypeStruct((N_EXPERTS, n_tm), F32),
        ],
        compiler_params=_params("arbitrary"),
        name="mix",
    )(x.reshape(N, D), ypool.reshape(N, POOL_WIDTH), ymla.reshape(N, H * V_HEAD), w_out[0].astype(BF16),
      ln1_g, ln1_b, wr_hi, wr_lo, router_bias.reshape(N_EXPERTS, 1), wsgu, w_sh_down[0].astype(BF16))

    c8 = tile_cnt.T.astype(I32)
    l_start = jnp.cumsum(c8, axis=1) - c8
    expert_rows = jnp.sum(c8, axis=0)
    padded = (expert_rows + BM - 1) // BM * BM
    pad_end = jnp.cumsum(padded)
    g_start = (pad_end - padded)[None, :] + jnp.cumsum(c8, axis=0) - c8
    n_blocks = -(-(N * TOP_K + n_tm * N_EXPERTS * (SEG - 1) + N_EXPERTS * (BM - 1)) // BM)
    P = n_blocks * BM
    blk_row = jnp.arange(n_blocks, dtype=I32) * BM
    block_e = jnp.minimum(jnp.sum((pad_end[None, :] <= blk_row[:, None]).astype(I32), axis=1), N_EXPERTS - 1)
    n_used = (pad_end[-1] // BM).astype(I32).reshape(1)
    n_big = c8 // CHUNK
    big_tab, big_cnt = _chunk_table(n_big, jnp.zeros_like(c8), CHUNK, MAX_BIG, l_start, g_start)
    small_tab, small_cnt = _chunk_table((c8 - n_big * CHUNK) // SEG, n_big * CHUNK, SEG, MAX_SMALL,
                                        l_start, g_start)
    seg_tables = (big_tab, small_tab, big_cnt, small_cnt)
    fill_start = jnp.concatenate([pad_end - padded + expert_rows, pad_end[-1:]]).astype(I32)
    fill_rows = jnp.concatenate([padded - expert_rows, P - pad_end[-1:]]).astype(I32)

    buf = pl.pallas_call(
        _dispatch_kernel,
        grid_spec=pltpu.PrefetchScalarGridSpec(
            num_scalar_prefetch=6, grid=(n_tm,),
            in_specs=[pl.BlockSpec((TOP_K, TM), lambda i, *_: (0, i)),
                      pl.BlockSpec((TM, D), lambda i, *_: (i, 0))],
            out_specs=pl.BlockSpec(memory_space=pl.ANY),
            scratch_shapes=[pltpu.VMEM((2, TILE_ROWS, HALF), U32), pltpu.VMEM((TILE_ROWS, TM), BF16),
                            pltpu.VMEM((ZERO_ROWS, HALF), U32),
                            pltpu.SemaphoreType.DMA((2,)), pltpu.SemaphoreType.DMA]),
        out_shape=jax.ShapeDtypeStruct((P, HALF), U32),
        compiler_params=_params("arbitrary"),
        name="dispatch",
    )(*seg_tables, fill_start, fill_rows, lpos, hb)

    def used(i, nu):
        return jnp.minimum(i, nu[0] - 1)

    obuf = pl.pallas_call(
        _ffn_kernel,
        grid_spec=pltpu.PrefetchScalarGridSpec(
            num_scalar_prefetch=2, grid=(n_blocks,),
            in_specs=[pl.BlockSpec((BM, HALF), lambda i, be, nu: (used(i, nu), 0)),
                      pl.BlockSpec((1, D, D_EXPERT), lambda i, be, nu: (be[used(i, nu)], 0, 0)),
                      pl.BlockSpec((1, D, D_EXPERT), lambda i, be, nu: (be[used(i, nu)], 0, 0)),
                      pl.BlockSpec((1, D_EXPERT, D), lambda i, be, nu: (be[used(i, nu)], 0, 0))],
            out_specs=pl.BlockSpec((BM, HALF), lambda i, be, nu: (i, 0)),
            scratch_shapes=[pltpu.VMEM((D, 2 * D_EXPERT), BF16), pltpu.VMEM((D_EXPERT, D), BF16)]),
        out_shape=jax.ShapeDtypeStruct((P, HALF), U32),
        compiler_params=_params("arbitrary"),
        name="ffn",
    )(block_e, n_used, buf, w_gate[0], w_up[0], w_down[0])

    out = pl.pallas_call(
        _combine_kernel,
        grid_spec=pltpu.PrefetchScalarGridSpec(
            num_scalar_prefetch=4, grid=(n_tm,),
            in_specs=[pl.BlockSpec((TOP_K, TM), lambda i, *_: (0, i)),
                      pl.BlockSpec((TOP_K, TM), lambda i, *_: (0, i)),
                      pl.BlockSpec((TM, D), lambda i, *_: (i, 0)),
                      pl.BlockSpec((1, D), lambda i, *_: (0, 0)), pl.BlockSpec((1, D), lambda i, *_: (0, 0)),
                      pl.BlockSpec(memory_space=pl.ANY)],
            out_specs=pl.BlockSpec((TM, D), lambda i, *_: (i, 0)),
            scratch_shapes=[pltpu.VMEM((2, TILE_ROWS, HALF), U32), pltpu.VMEM((TILE_ROWS, TM), BF16),
                            pltpu.SemaphoreType.DMA((2,))]),
        out_shape=jax.ShapeDtypeStruct((N, D), F32),
        compiler_params=_params("arbitrary"),
        name="combine",
    )(*seg_tables, lpos, top_w, base, ln2_g, ln2_b, obuf)
    return out.reshape(B, S, D)
```
